```python
import numpy as np
import jax
import jax.numpy as jnp
from jax import lax

D_MODEL = 1024
BATCH = 8
SEQ = 2048
DEPTH = 4
DEC_BATCH = 128
DEC_SEQ = 4
PAST_LEN = 16384
PAGE_SIZE = 128

HG_HEADS = 4
HG_DK = 128
HG_DV = 128
GLA_HEADS = 4
GLA_DK = 64
GLA_DV = 128
GLA_RANK = 16
GLA_GATE_NORM = 16.0
ML_HEADS = 4
ML_DK = 128
ML_DV = 128
N_BRANCH = 3
BRANCH_W = 512
D_FF = 2816
CHUNK = 64
EPS = 1e-6
NEG_BIG = -1e30

_SPLIT_SIZES = (
    HG_HEADS * HG_DK, HG_HEADS * HG_DK, HG_HEADS * HG_DV, HG_HEADS * HG_DV,
    GLA_HEADS * GLA_DK, GLA_HEADS * GLA_DK, GLA_HEADS * GLA_DV, GLA_HEADS * GLA_DV,
    GLA_RANK,
    ML_HEADS * ML_DK, ML_HEADS * ML_DK, ML_HEADS * ML_DV, ML_HEADS * ML_DV,
    ML_HEADS, ML_HEADS,
    N_BRANCH * D_MODEL,
)
D_IN = int(sum(_SPLIT_SIZES))
_SPLIT_IDX = [int(v) for v in np.cumsum(_SPLIT_SIZES)[:-1]]

kernel_name = "hybrid_hgrn2_gla_mlstm_macaron_step"


def rmsnorm(x, g):
    xf = x.astype(jnp.float32)
    y = xf * lax.rsqrt(jnp.mean(xf * xf, axis=-1, keepdims=True) + EPS)
    return (y * g.astype(jnp.float32)).astype(x.dtype)


def head_rmsnorm(o, g):
    return o * lax.rsqrt(jnp.mean(o * o, axis=-1, keepdims=True) + EPS) * g.astype(jnp.float32)


def split_heads(t, h):
    b, l, _ = t.shape
    return t.reshape(b, l, h, -1).transpose(0, 2, 1, 3)


def merge_heads(t):
    b, h, l, d = t.shape
    return t.transpose(0, 2, 1, 3).reshape(b, l, h * d)


def chunk_len(length):
    return CHUNK if length % CHUNK == 0 else length


def to_chunks(t, c):
    b, h, l = t.shape[:3]
    t = t.reshape((b, h, l // c, c) + t.shape[3:])
    return jnp.moveaxis(t, 2, 0)


def from_chunks(t):
    t = jnp.moveaxis(t, 0, 2)
    b, h, n, c = t.shape[:4]
    return t.reshape((b, h, n * c) + t.shape[4:])


def gated_linear_scan(q, k, v, logf, s0):
    c = chunk_len(q.shape[2])
    tri = jnp.tril(jnp.ones((c, c), dtype=bool))

    def step(s, inp):
        qc, kc, vc, gc = inp
        b = jnp.cumsum(gc, axis=2)
        diff = b[:, :, :, None, :] - b[:, :, None, :, :]
        decay = jnp.exp(jnp.where(tri[:, :, None], diff, NEG_BIG))
        att = jnp.einsum('bhtd,bhsd,bhtsd->bhts', qc, kc, decay)
        o = (jnp.einsum('bhts,bhsv->bhtv', att, vc)
             + jnp.einsum('bhtd,bhdv->bhtv', qc * jnp.exp(b), s))
        b_end = b[:, :, -1:, :]
        s_new = (jnp.exp(b_end[:, :, 0, :])[..., None] * s
                 + jnp.einsum('bhsd,bhsv->bhdv', kc * jnp.exp(b_end - b), vc))
        return s_new, o

    s_fin, o = lax.scan(step, s0, (to_chunks(q, c), to_chunks(k, c), to_chunks(v, c), to_chunks(logf, c)))
    return from_chunks(o), s_fin


def mlstm_scan(q, k, v, logi, logf, c0, n0, m0):
    c = chunk_len(q.shape[2])
    tri = jnp.tril(jnp.ones((c, c), dtype=bool))

    def step(carry, inp):
        cst, n, m = carry
        qc, kc, vc, ic, fc = inp
        b = jnp.cumsum(fc, axis=-1)
        log_d = jnp.where(tri, b[..., :, None] - b[..., None, :] + ic[..., None, :], NEG_BIG)
        inter = b + m[..., None]
        m_t = jnp.maximum(inter, jnp.max(log_d, axis=-1))
        d = jnp.exp(log_d - m_t[..., None])
        w_inter = jnp.exp(inter - m_t)
        qk = jnp.einsum('bhtd,bhsd->bhts', qc, kc) * d
        num = (jnp.einsum('bhts,bhsv->bhtv', qk, vc)
               + w_inter[..., None] * jnp.einsum('bhtd,bhdv->bhtv', qc, cst))
        den = jnp.sum(qk, axis=-1) + w_inter * jnp.einsum('bhtd,bhd->bht', qc, n)
        h = num / jnp.maximum(jnp.abs(den), jnp.exp(-m_t))[..., None]
        m_new = m_t[..., -1]
        w_s = jnp.exp(b[..., -1:] - b + ic - m_new[..., None])
        carry_decay = jnp.exp(b[..., -1] + m - m_new)
        c_new = carry_decay[..., None, None] * cst + jnp.einsum('bhs,bhsd,bhsv->bhdv', w_s, kc, vc)
        n_new = carry_decay[..., None] * n + jnp.einsum('bhs,bhsd->bhd', w_s, kc)
        return (c_new, n_new, m_new), h

    (c_f, n_f, m_f), h = lax.scan(
        step, (c0, n0, m0),
        (to_chunks(q, c), to_chunks(k, c), to_chunks(v, c), to_chunks(logi, c), to_chunks(logf, c)))
    return from_chunks(h), c_f, n_f, m_f


def swiglu_half(x, g, w_up, w_down):
    h = rmsnorm(x, g)
    gate, up = jnp.split(h @ w_up, 2, axis=-1)
    return x + 0.5 * ((jax.nn.silu(gate) * up) @ w_down)


def mixer(h, l, lb, p, st):
    dt = h.dtype
    f32 = lambda t: t.astype(jnp.float32)
    s_hg, s_gla, c_ml, n_ml, m_ml = [f32(s) for s in st]
    (hq, hf, hi, hg, gq, gk, gv, gg, glr, mq, mk, mv, mo, mi, mf, mg) = jnp.split(
        h @ p['w_in'][l], _SPLIT_IDX, axis=-1)

    zf = f32(hf)
    f_hg = lb + (1.0 - lb) * jax.nn.sigmoid(zf)
    log_f = jnp.log(f_hg)
    k_hg = (1.0 - lb) * jax.nn.sigmoid(-zf)
    o_hg, s_hg_new = gated_linear_scan(
        split_heads(jax.nn.silu(f32(hq)), HG_HEADS), split_heads(k_hg, HG_HEADS),
        split_heads(f32(hi), HG_HEADS), split_heads(log_f, HG_HEADS), s_hg)
    o_hg = merge_heads(head_rmsnorm(o_hg, p['hgrn_out_norm'][l])) * jax.nn.silu(f32(hg))

    log_a = jax.nn.log_sigmoid(f32(glr @ p['gla_w_gate_lr'][l]) + f32(p['gla_b_gate'][l])) / GLA_GATE_NORM
    o_gla, s_gla_new = gated_linear_scan(
        split_heads(f32(gq) * (GLA_DK ** -0.5), GLA_HEADS), split_heads(f32(gk), GLA_HEADS),
        split_heads(f32(gv), GLA_HEADS), split_heads(log_a, GLA_HEADS), s_gla)
    o_gla = merge_heads(head_rmsnorm(o_gla, p['gla_out_norm'][l])) * jax.nn.silu(f32(gg))

    logi = (f32(mi) + f32(p['mlstm_b_i'][l])).transpose(0, 2, 1)
    logfm = jax.nn.log_sigmoid(f32(mf) + f32(p['mlstm_b_f'][l])).transpose(0, 2, 1)
    h_ml, c_new, n_new, m_new = mlstm_scan(
        split_heads(f32(mq), ML_HEADS), split_heads(f32(mk) * (ML_DK ** -0.5), ML_HEADS),
        split_heads(f32(mv), ML_HEADS), logi, logfm, c_ml, n_ml, m_ml)
    o_ml = merge_heads(head_rmsnorm(h_ml, p['mlstm_out_norm'][l])) * jax.nn.sigmoid(f32(mo))

    b_, l_, _ = h.shape
    branches = jnp.stack([o_hg, o_gla, o_ml], axis=2).astype(dt)
    proj = jnp.einsum('blcw,cwd->blcd', branches, p['w_branch'][l])
    gates = jax.nn.sigmoid(mg).reshape(b_, l_, N_BRANCH, D_MODEL)
    merged = jnp.einsum('blcd,blcd->bld', gates, proj)
    y = merged @ p['w_out'][l]
    return y, (s_hg_new, s_gla_new, c_new, n_new, m_new)


def trunk(x, states, p):
    lb_soft = jax.nn.softmax(p['hgrn_lb_raw'].astype(jnp.float32), axis=0)
    lb_all = jnp.cumsum(lb_soft, axis=0) - lb_soft[0]
    outs = ([], [], [], [], [])
    for l in range(DEPTH):
        x = swiglu_half(x, p['ffn1_norm'][l], p['ffn1_w_up'][l], p['ffn1_w_down'][l])
        y, st = mixer(rmsnorm(x, p['mix_norm'][l]), l, lb_all[l], p, tuple(s[l] for s in states))
        x = x + y
        x = swiglu_half(x, p['ffn2_norm'][l], p['ffn2_w_up'][l], p['ffn2_w_down'][l])
        for acc, s in zip(outs, st):
            acc.append(s)
    return rmsnorm(x, p['final_norm']), tuple(jnp.stack(a) for a in outs)


def setup_inputs(seed: int = 0) -> dict:
    key = jax.random.key(seed)
    ks = jax.random.split(key, 32)
    nrm = lambda k, shape, scale: jax.random.normal(k, shape, jnp.float32) * scale
    gain = lambda k, shape: 1.0 + nrm(k, shape, 0.02)
    hd = (DEPTH, DEC_BATCH)
    return {
        "x_prompt": nrm(ks[0], (BATCH, SEQ, D_MODEL), 1.0),
        "x_sample": nrm(ks[1], (DEC_BATCH, DEC_SEQ, D_MODEL), 1.0),
        "state_hgrn": nrm(ks[2], hd + (HG_HEADS, HG_DK, HG_DV), 0.5),
        "state_gla": nrm(ks[3], hd + (GLA_HEADS, GLA_DK, GLA_DV), 1.0),
        "state_mlstm_C": nrm(ks[4], hd + (ML_HEADS, ML_DK, ML_DV), 0.5),
        "state_mlstm_n": nrm(ks[5], hd + (ML_HEADS, ML_DK), 0.5),
        "state_mlstm_m": nrm(ks[6], hd + (ML_HEADS,), 1.0),
        "ffn1_norm": gain(ks[7], (DEPTH, D_MODEL)),
        "ffn1_w_up": nrm(ks[8], (DEPTH, D_MODEL, 2 * D_FF), D_MODEL ** -0.5),
        "ffn1_w_down": nrm(ks[9], (DEPTH, D_FF, D_MODEL), D_FF ** -0.5),
        "mix_norm": gain(ks[10], (DEPTH, D_MODEL)),
        "w_in": nrm(ks[11], (DEPTH, D_MODEL, D_IN), D_MODEL ** -0.5),
        "hgrn_lb_raw": nrm(ks[12], (DEPTH, HG_HEADS * HG_DK), 0.1),
        "hgrn_out_norm": gain(ks[13], (DEPTH, HG_DV)),
        "gla_w_gate_lr": nrm(ks[14], (DEPTH, GLA_RANK, GLA_HEADS * GLA_DK), GLA_RANK ** -0.5),
        "gla_b_gate": nrm(ks[15], (DEPTH, GLA_HEADS * GLA_DK), 0.1),
        "gla_out_norm": gain(ks[16], (DEPTH, GLA_DV)),
        "mlstm_b_i": nrm(ks[17], (DEPTH, ML_HEADS), 0.1),
        "mlstm_b_f": jnp.linspace(3.0, 6.0, ML_HEADS, dtype=jnp.float32)[None, :] + nrm(ks[18], (DEPTH, ML_HEADS), 0.1),
        "mlstm_out_norm": gain(ks[19], (DEPTH, ML_DV)),
        "w_branch": nrm(ks[20], (DEPTH, N_BRANCH, BRANCH_W, D_MODEL), BRANCH_W ** -0.5),
        "w_out": nrm(ks[21], (DEPTH, D_MODEL, D_MODEL), D_MODEL ** -0.5),
        "ffn2_norm": gain(ks[22], (DEPTH, D_MODEL)),
        "ffn2_w_up": nrm(ks[23], (DEPTH, D_MODEL, 2 * D_FF), D_MODEL ** -0.5),
        "ffn2_w_down": nrm(ks[24], (DEPTH, D_FF, D_MODEL), D_FF ** -0.5),
        "final_norm": gain(ks[25], (D_MODEL,)),
    }


def reference(x_prompt, x_sample, state_hgrn, state_gla, state_mlstm_C, state_mlstm_n, state_mlstm_m,
              ffn1_norm, ffn1_w_up, ffn1_w_down, mix_norm, w_in, hgrn_lb_raw, hgrn_out_norm,
              gla_w_gate_lr, gla_b_gate, gla_out_norm, mlstm_b_i, mlstm_b_f, mlstm_out_norm,
              w_branch, w_out, ffn2_norm, ffn2_w_up, ffn2_w_down, final_norm):
    p = dict(ffn1_norm=ffn1_norm, ffn1_w_up=ffn1_w_up, ffn1_w_down=ffn1_w_down, mix_norm=mix_norm,
             w_in=w_in, hgrn_lb_raw=hgrn_lb_raw, hgrn_out_norm=hgrn_out_norm,
             gla_w_gate_lr=gla_w_gate_lr, gla_b_gate=gla_b_gate, gla_out_norm=gla_out_norm,
             mlstm_b_i=mlstm_b_i, mlstm_b_f=mlstm_b_f, mlstm_out_norm=mlstm_out_norm,
             w_branch=w_branch, w_out=w_out, ffn2_norm=ffn2_norm, ffn2_w_up=ffn2_w_up,
             ffn2_w_down=ffn2_w_down, final_norm=final_norm)
    bp = (DEPTH, BATCH)
    zero_states = (
        jnp.zeros(bp + (HG_HEADS, HG_DK, HG_DV), jnp.float32),
        jnp.zeros(bp + (GLA_HEADS, GLA_DK, GLA_DV), jnp.float32),
        jnp.zeros(bp + (ML_HEADS, ML_DK, ML_DV), jnp.float32),
        jnp.zeros(bp + (ML_HEADS, ML_DK), jnp.float32),
        jnp.zeros(bp + (ML_HEADS,), jnp.float32),
    )
    y_prompt, ps = trunk(x_prompt, zero_states, p)
    y_sample, ss = trunk(x_sample, (state_hgrn, state_gla, state_mlstm_C, state_mlstm_n, state_mlstm_m), p)
    return (y_prompt, y_sample, ps[0], ps[1], ps[2], ps[3], ps[4], ss[0], ss[1], ss[2], ss[3], ss[4])
```

```python
import functools

import numpy as np
import jax
import jax.numpy as jnp
from jax import lax
from jax.experimental import pallas as pl
from jax.experimental.pallas import tpu as pltpu

D_MODEL = 1024
HEADS = 4
HG_DK = 128
GLA_DK = 64
GLA_RANK = 16
GLA_GATE_NORM = 16.0
ML_DK = 128
DV = 128
BRANCH_W = 512
N_BRANCH = 3
D_FF = 2816
CHUNK = 64
EPS = 1e-6
NEG_BIG = -1e30

F32 = jnp.float32
BF16 = jnp.bfloat16

COL_HG = 0
COL_ML = 2048
COL_GATE = 4096
COL_GLA_QK = 7168
COL_GLA_V = 7680
COL_GLA_G = 8192
COL_SMALL = 8704
P_COLS = 8832
SMALL_W = 128
SM_I = GLA_RANK
SM_F = GLA_RANK + HEADS

TM_FFN = 256
TM_TOK = 512
PROJ_COL_TILE = 2944
SAMPLE_NB = 8
VMEM_LIMIT = 56 * 1024 * 1024


def _cparams(sem):
    return pltpu.CompilerParams(dimension_semantics=sem, vmem_limit_bytes=VMEM_LIMIT)


def _dot(a, b):
    return jnp.dot(a, b, preferred_element_type=F32)


def _dot_nt(a, b):
    return lax.dot_general(a, b, (((1,), (1,)), ((), ())), preferred_element_type=F32)


def _dot_tn(a, b):
    return lax.dot_general(a, b, (((0,), (0,)), ((), ())), preferred_element_type=F32)


def _rms(x, g):
    return x * lax.rsqrt(jnp.mean(x * x, axis=-1, keepdims=True) + EPS) * g


def _log_sigmoid(x):
    return jnp.minimum(x, 0.0) - jnp.log1p(jnp.exp(-jnp.abs(x)))


def _silu(x):
    return x * jax.nn.sigmoid(x)


def _exact_dot(m_bf16, x):
    hi = x.astype(BF16)
    r1 = x - hi.astype(F32)
    mid = r1.astype(BF16)
    lo = (r1 - mid.astype(F32)).astype(BF16)
    return _dot(m_bf16, hi) + _dot(m_bf16, mid) + _dot(m_bf16, lo)


def _level_sizes(c):
    out, m = [], c // 2
    while m >= 1:
        out.append(m)
        m //= 2
    return out


def _decay_matrix(c):
    blocks = []
    for m in _level_sizes(c):
        mat = np.zeros((c, c), np.float32)
        for t in range(c):
            mid = (t // (2 * m)) * (2 * m) + m
            if t >= mid:
                mat[t, mid:t + 1] = 1.0
            else:
                mat[t, t + 1:mid] = 1.0
        blocks.append(mat)
    blocks.append(np.tril(np.ones((c, c), np.float32)))
    blocks.append(np.triu(np.ones((c, c), np.float32), 1))
    return np.concatenate(blocks, axis=0)


def _eye(n):
    return lax.broadcasted_iota(jnp.int32, (n, n), 0) == lax.broadcasted_iota(jnp.int32, (n, n), 1)


def _pair_masks(c):
    ti = lax.broadcasted_iota(jnp.int32, (c, c), 0)
    si = lax.broadcasted_iota(jnp.int32, (c, c), 1)
    levels = []
    for m in _level_sizes(c):
        same = (ti // (2 * m)) == (si // (2 * m))
        levels.append(same & ((ti & m) != 0) & ((si & m) == 0))
    return levels, ti == si, si <= ti


def _column_of(row, eye):
    return jnp.sum(jnp.where(eye, row, 0.0), axis=1, keepdims=True)


def _row_of(col, eye):
    return jnp.sum(jnp.where(eye, col, 0.0), axis=0, keepdims=True)


def _ffn_body(x_ref, g_ref, wup_ref, wdn_ref, fin_ref, o_ref, *, final):
    x = x_ref[...]
    h = _rms(x, g_ref[...]).astype(BF16)
    gu = _dot(h, wup_ref[...])
    act = _silu(gu[:, :D_FF]) * gu[:, D_FF:]
    out = x + 0.5 * _dot(act.astype(BF16), wdn_ref[...])
    if final:
        out = _rms(out, fin_ref[...])
    o_ref[...] = out


def _ffn(x, norm, w_up, w_down, final_norm, layer, final):
    t = x.shape[0]
    return pl.pallas_call(
        functools.partial(_ffn_body, final=final),
        grid=(t // TM_FFN,),
        in_specs=[
            pl.BlockSpec((TM_FFN, D_MODEL), lambda i: (i, 0)),
            pl.BlockSpec((None, 1, D_MODEL), lambda i: (layer, 0, 0)),
            pl.BlockSpec((None, D_MODEL, 2 * D_FF), lambda i: (layer, 0, 0)),
            pl.BlockSpec((None, D_FF, D_MODEL), lambda i: (layer, 0, 0)),
            pl.BlockSpec((1, D_MODEL), lambda i: (0, 0)),
        ],
        out_specs=pl.BlockSpec((TM_FFN, D_MODEL), lambda i: (i, 0)),
        out_shape=jax.ShapeDtypeStruct((t, D_MODEL), F32),
        compiler_params=_cparams(("arbitrary",)),
        name="ffn",
    )(x, norm, w_up, w_down, final_norm)


def _inproj_body(x_ref, g_ref, w_ref, o_ref):
    h = _rms(x_ref[...], g_ref[...]).astype(BF16)
    o_ref[...] = _dot(h, w_ref[...])


def _inproj(x, norm, w_all, layer):
    t = x.shape[0]
    return pl.pallas_call(
        _inproj_body,
        grid=(P_COLS // PROJ_COL_TILE, t // TM_TOK),
        in_specs=[
            pl.BlockSpec((TM_TOK, D_MODEL), lambda j, i: (i, 0)),
            pl.BlockSpec((None, 1, D_MODEL), lambda j, i: (layer, 0, 0)),
            pl.BlockSpec((None, D_MODEL, PROJ_COL_TILE), lambda j, i: (layer, 0, j)),
        ],
        out_specs=pl.BlockSpec((TM_TOK, PROJ_COL_TILE), lambda j, i: (i, j)),
        out_shape=jax.ShapeDtypeStruct((t, P_COLS), F32),
        compiler_params=_cparams(("arbitrary", "arbitrary")),
        name="inproj",
    )(x, norm, w_all)


def _merge_body(x_ref, op_ref, os_ref, g0_ref, g1_ref, g2_ref, wb_ref, wo_ref, o_ref, *, n_prompt_tiles):
    is_prompt = pl.program_id(0) < n_prompt_tiles
    merged = None
    for c, g_ref in enumerate((g0_ref, g1_ref, g2_ref)):
        cs = slice(c * BRANCH_W, (c + 1) * BRANCH_W)
        br = jnp.where(is_prompt, op_ref[:, cs], os_ref[:, cs]).astype(BF16)
        term = jax.nn.sigmoid(g_ref[...]) * _dot(br, wb_ref[c])
        merged = term if merged is None else merged + term
    o_ref[...] = x_ref[...] + _dot(merged.astype(BF16), wo_ref[...])


def _merge(x, o_prompt, o_sample, p_all, w_branch, w_out, layer):
    t = x.shape[0]
    n_prompt_tiles = o_prompt.shape[0] // TM_TOK
    gate_blk = COL_GATE // D_MODEL

    def gate_spec(c):
        return pl.BlockSpec((TM_TOK, D_MODEL), lambda i: (i, gate_blk + c))

    return pl.pallas_call(
        functools.partial(_merge_body, n_prompt_tiles=n_prompt_tiles),
        grid=(t // TM_TOK,),
        in_specs=[
            pl.BlockSpec((TM_TOK, D_MODEL), lambda i: (i, 0)),
            pl.BlockSpec((TM_TOK, N_BRANCH * BRANCH_W), lambda i: (jnp.minimum(i, n_prompt_tiles - 1), 0)),
            pl.BlockSpec((TM_TOK, N_BRANCH * BRANCH_W), lambda i: (0, 0)),
            gate_spec(0), gate_spec(1), gate_spec(2),
            pl.BlockSpec((None, N_BRANCH, BRANCH_W, D_MODEL), lambda i: (layer, 0, 0, 0)),
            pl.BlockSpec((None, D_MODEL, D_MODEL), lambda i: (layer, 0, 0)),
        ],
        out_specs=pl.BlockSpec((TM_TOK, D_MODEL), lambda i: (i, 0)),
        out_shape=jax.ShapeDtypeStruct((t, D_MODEL), F32),
        compiler_params=_cparams(("arbitrary",)),
        name="merge",
    )(x, o_prompt, o_sample, p_all, p_all, p_all, w_branch, w_out)


def _layer_lower_bound(lb_raw, layer):
    e = jnp.exp(lb_raw - jnp.max(lb_raw, axis=0, keepdims=True))
    soft = e / jnp.sum(e, axis=0, keepdims=True)
    lb = jnp.zeros_like(soft[0:1])
    for j in range(1, layer + 1):
        lb = lb + soft[j:j + 1]
    return lb


def _head_norm(o, g):
    return o * lax.rsqrt(jnp.mean(o * o, axis=-1, keepdims=True) + EPS) * g


def _gated_linear_chunk(q, k, v, logf, dmat, masks, dk, read_state, write_state, emit):
    c = q.shape[0]
    level_masks, eye, _ = masks
    n_lev = len(level_masks)
    eye_dk = _eye(dk)
    e_all = _exact_dot(dmat, logf)
    for h in range(HEADS):
        ks = slice(h * dk, (h + 1) * dk)
        qh, kh, vh = q[:, ks], k[:, ks], v[:, h * DV:(h + 1) * DV]
        att = jnp.where(eye, _dot_nt(qh.astype(BF16), kh.astype(BF16)), 0.0)
        for lv in range(n_lev):
            a = jnp.exp(e_all[lv * c:(lv + 1) * c, ks])
            prod = _dot_nt((qh * a).astype(BF16), (kh * a).astype(BF16))
            att = att + jnp.where(level_masks[lv], prod, 0.0)
        cum = e_all[n_lev * c:(n_lev + 1) * c, ks]
        rev = e_all[(n_lev + 1) * c:(n_lev + 2) * c, ks]
        st = read_state(h)
        emit(h, _dot(att.astype(BF16), vh.astype(BF16))
             + _dot((qh * jnp.exp(cum)).astype(BF16), st.astype(BF16)))
        decay = _column_of(jnp.exp(cum[c - 1:c, :]), eye_dk)
        write_state(h, st * decay + _dot_tn((kh * jnp.exp(rev)).astype(BF16), vh.astype(BF16)))


def _mlstm_chunk(q, k, v, gates, tri_bf16, masks, read_state, write_state, emit):
    c = q.shape[0]
    _, eye, tri = masks
    bcum = _exact_dot(tri_bf16, gates)
    for h in range(HEADS):
        sl = slice(h * ML_DK, (h + 1) * ML_DK)
        qh, kh, vh = q[:, sl], k[:, sl] * (ML_DK ** -0.5), v[:, sl]
        bcol = bcum[:, SM_F + h:SM_F + h + 1]
        icol = gates[:, SM_I + h:SM_I + h + 1]
        log_d = jnp.where(tri, bcol + _row_of(icol - bcol, eye), NEG_BIG)
        cst, nrow, m_prev = read_state(h)
        inter = bcol + m_prev
        m_t = jnp.maximum(inter, jnp.max(log_d, axis=1, keepdims=True))
        d = jnp.exp(log_d - m_t)
        w_inter = jnp.exp(inter - m_t)
        qk = _dot_nt(qh.astype(BF16), kh.astype(BF16)) * d
        num = _dot(qk.astype(BF16), vh.astype(BF16)) + w_inter * _dot(qh.astype(BF16), cst.astype(BF16))
        den = jnp.sum(qk, axis=1, keepdims=True) + w_inter * jnp.sum(qh * nrow, axis=1, keepdims=True)
        emit(h, num / jnp.maximum(jnp.abs(den), jnp.exp(-m_t)))
        m_new = m_t[c - 1:c, :]
        b_end = bcol[c - 1:c, :]
        kw = jnp.exp(b_end - bcol + icol - m_new) * kh
        carry = jnp.exp(b_end + m_prev - m_new)
        write_state(h,
                    carry * cst + _dot_tn(kw.astype(BF16), vh.astype(BF16)),
                    carry * nrow + jnp.sum(kw, axis=0, keepdims=True),
                    m_new)


def _sequence_chunk(blocks, prm, dmat, state_io, emit):
    p_hg, p_ml, p_gqk, p_gv, p_gg, p_small = blocks
    lb, hg_norm, wlr, gla_b, gla_norm, ml_bias, ml_norm = prm
    hg_io, gla_io, ml_io = state_io
    c = p_hg.shape[0]
    masks = _pair_masks(c)
    n_lev = len(masks[0])
    w = HEADS * HG_DK

    zf = p_hg[:, w:2 * w]
    logf = jnp.log(lb + (1.0 - lb) * jax.nn.sigmoid(zf))
    k_hg = (1.0 - lb) * jax.nn.sigmoid(-zf)
    g_hg = p_hg[:, 3 * w:4 * w]
    _gated_linear_chunk(
        _silu(p_hg[:, 0:w]), k_hg, p_hg[:, 2 * w:3 * w], logf, dmat, masks, HG_DK, hg_io[0], hg_io[1],
        lambda h, o: emit(0, h, _head_norm(o, hg_norm) * _silu(g_hg[:, h * DV:(h + 1) * DV])))

    wg = HEADS * GLA_DK
    log_a = _log_sigmoid(_dot(p_small.astype(BF16), wlr) + gla_b) / GLA_GATE_NORM
    _gated_linear_chunk(
        p_gqk[:, 0:wg] * (GLA_DK ** -0.5), p_gqk[:, wg:2 * wg], p_gv, log_a, dmat, masks, GLA_DK,
        gla_io[0], gla_io[1],
        lambda h, o: emit(1, h, _head_norm(o, gla_norm) * _silu(p_gg[:, h * DV:(h + 1) * DV])))

    lane = lax.broadcasted_iota(jnp.int32, p_small.shape, 1)
    biased = p_small + ml_bias
    gates = jnp.where(lane >= SM_F, _log_sigmoid(biased), biased)
    o_ml = p_ml[:, 3 * w:4 * w]
    _mlstm_chunk(
        p_ml[:, 0:w], p_ml[:, w:2 * w], p_ml[:, 2 * w:3 * w], gates, dmat[n_lev * c:(n_lev + 1) * c, :],
        masks, ml_io[0], ml_io[1],
        lambda h, o: emit(2, h, _head_norm(o, ml_norm) * jax.nn.sigmoid(o_ml[:, h * DV:(h + 1) * DV])))


def _load_params(param_refs, layer):
    lb_ref, *rest = param_refs
    return (_layer_lower_bound(lb_ref[...], layer),) + tuple(r[...] for r in rest)


def _state_io(b, in_refs, out_refs):
    ihg, igla, imc, imn, imm = in_refs
    ohg, ogla, omc, omn, omm = out_refs

    def write_to(ref):
        def write(h, s):
            ref[b, h] = s
        return write

    def ml_write(h, c_new, n_new, m_new):
        omc[b, h] = c_new
        omn[b, h] = n_new
        omm[b, h] = jnp.broadcast_to(m_new, (1, SMALL_W))

    return ((lambda h: ihg[b, h], write_to(ohg)),
            (lambda h: igla[b, h], write_to(ogla)),
            (lambda h: (imc[b, h], imn[b, h], imm[b, h][:, 0:1]), ml_write))


N_BLOCKS = 6
N_PARAMS = 7
STATE_TAILS = [(HG_DK, DV), (GLA_DK, DV), (ML_DK, DV), (1, ML_DK), (1, SMALL_W)]


def _prompt_scan_body(*refs, layer, batch):
    block_refs = refs[:N_BLOCKS]
    param_refs = refs[N_BLOCKS:N_BLOCKS + N_PARAMS]
    dmat_ref, o_ref = refs[N_BLOCKS + N_PARAMS:N_BLOCKS + N_PARAMS + 2]
    state_refs = refs[N_BLOCKS + N_PARAMS + 2:]

    @pl.when(pl.program_id(0) == 0)
    def _():
        for r in state_refs:
            r[...] = jnp.zeros_like(r)

    prm = _load_params(param_refs, layer)
    dmat = dmat_ref[...]

    def per_sequence(b, carry):
        rows = pl.ds(pl.multiple_of(b * CHUNK, CHUNK), CHUNK)

        def emit(branch, h, val):
            c0 = branch * BRANCH_W + h * DV
            o_ref[rows, c0:c0 + DV] = val

        _sequence_chunk([r[rows, :] for r in block_refs], prm, dmat, _state_io(b, state_refs, state_refs), emit)
        return carry

    lax.fori_loop(0, batch, per_sequence, 0)


def _param_specs(layer, depth):
    return [
        pl.BlockSpec((depth, HEADS * HG_DK), lambda *_: (0, 0)),
        pl.BlockSpec((None, 1, DV), lambda *_: (layer, 0, 0)),
        pl.BlockSpec((None, SMALL_W, HEADS * GLA_DK), lambda *_: (layer, 0, 0)),
        pl.BlockSpec((None, 1, HEADS * GLA_DK), lambda *_: (layer, 0, 0)),
        pl.BlockSpec((None, 1, DV), lambda *_: (layer, 0, 0)),
        pl.BlockSpec((None, 1, SMALL_W), lambda *_: (layer, 0, 0)),
        pl.BlockSpec((None, 1, DV), lambda *_: (layer, 0, 0)),
    ]


def _prompt_scan(p_all, params, dmat, layer, depth, batch, n_chunks):
    rows = batch * CHUNK

    def col(width, start):
        return pl.BlockSpec((rows, width), lambda c: (c, start // width))

    full = lambda shape: pl.BlockSpec(shape, lambda c: (0,) * len(shape))
    state_shapes = [(batch, HEADS) + t for t in STATE_TAILS]
    return pl.pallas_call(
        functools.partial(_prompt_scan_body, layer=layer, batch=batch),
        grid=(n_chunks,),
        in_specs=[col(2048, COL_HG), col(2048, COL_ML), col(512, COL_GLA_QK), col(512, COL_GLA_V),
                  col(512, COL_GLA_G), col(SMALL_W, COL_SMALL)] + _param_specs(layer, depth) + [full(dmat.shape)],
        out_specs=[pl.BlockSpec((rows, N_BRANCH * BRANCH_W), lambda c: (c, 0))] + [full(s) for s in state_shapes],
        out_shape=[jax.ShapeDtypeStruct((n_chunks * rows, N_BRANCH * BRANCH_W), F32)]
        + [jax.ShapeDtypeStruct(s, F32) for s in state_shapes],
        compiler_params=_cparams(("arbitrary",)),
        name="prompt_scan",
    )(*([p_all] * N_BLOCKS), *params, dmat)


def _sample_scan_body(*refs, layer):
    block_refs = refs[:N_BLOCKS]
    param_refs = refs[N_BLOCKS:N_BLOCKS + N_PARAMS]
    dmat_ref = refs[N_BLOCKS + N_PARAMS]
    n_state = len(STATE_TAILS)
    in_state_refs = refs[N_BLOCKS + N_PARAMS + 1:N_BLOCKS + N_PARAMS + 1 + n_state]
    o_ref = refs[N_BLOCKS + N_PARAMS + 1 + n_state]
    out_state_refs = refs[N_BLOCKS + N_PARAMS + 2 + n_state:]

    prm = _load_params(param_refs, layer)
    dmat = dmat_ref[...]

    def per_sequence(b, carry):
        def emit(branch, h, val):
            c0 = branch * BRANCH_W + h * DV
            o_ref[b, :, c0:c0 + DV] = val

        _sequence_chunk([r[b] for r in block_refs], prm, dmat, _state_io(b, in_state_refs, out_state_refs), emit)
        return carry

    lax.fori_loop(0, o_ref.shape[0], per_sequence, 0)


def _sample_scan(p_s, params, dmat, states, layer, depth):
    n_seq, seq, _ = p_s.shape
    nb = SAMPLE_NB

    def col(width, start):
        return pl.BlockSpec((nb, seq, width), lambda i: (i, 0, start // width))

    full = lambda shape: pl.BlockSpec(shape, lambda i: (0,) * len(shape))
    return pl.pallas_call(
        functools.partial(_sample_scan_body, layer=layer),
        grid=(n_seq // nb,),
        in_specs=[col(2048, COL_HG), col(2048, COL_ML), col(512, COL_GLA_QK), col(512, COL_GLA_V),
                  col(512, COL_GLA_G), col(SMALL_W, COL_SMALL)] + _param_specs(layer, depth) + [full(dmat.shape)]
        + [pl.BlockSpec((None, nb, HEADS) + t, lambda i: (layer, i, 0, 0, 0)) for t in STATE_TAILS],
        out_specs=[pl.BlockSpec((nb, seq, N_BRANCH * BRANCH_W), lambda i: (i, 0, 0))]
        + [pl.BlockSpec((nb, HEADS) + t, lambda i: (i, 0, 0, 0)) for t in STATE_TAILS],
        out_shape=[jax.ShapeDtypeStruct((n_seq, seq, N_BRANCH * BRANCH_W), F32)]
        + [jax.ShapeDtypeStruct((n_seq, HEADS) + t, F32) for t in STATE_TAILS],
        compiler_params=_cparams(("arbitrary",)),
        name="sample_scan",
    )(*([p_s] * N_BLOCKS), *params, dmat, *states)


def _regroup_w_in(w_in):
    hg, gla, lr, ml, gates_if, mg = (w_in[..., 0:2048], w_in[..., 2048:3584], w_in[..., 3584:3600],
                                     w_in[..., 3600:5648], w_in[..., 5648:5656], w_in[..., 5656:8728])
    pad = jnp.zeros(w_in.shape[:-1] + (SMALL_W - GLA_RANK - 2 * HEADS,), w_in.dtype)
    return jnp.concatenate([hg, ml, mg, gla, lr, gates_if, pad], axis=-1).astype(BF16)


def kernel(x_prompt, x_sample, state_hgrn, state_gla, state_mlstm_C, state_mlstm_n, state_mlstm_m,
           ffn1_norm, ffn1_w_up, ffn1_w_down, mix_norm, w_in, hgrn_lb_raw, hgrn_out_norm,
           gla_w_gate_lr, gla_b_gate, gla_out_norm, mlstm_b_i, mlstm_b_f, mlstm_out_norm,
           w_branch, w_out, ffn2_norm, ffn2_w_up, ffn2_w_down, final_norm):
    depth = w_in.shape[0]
    batch, seq, _ = x_prompt.shape
    n_seq, dec_seq, _ = x_sample.shape
    assert seq % CHUNK == 0 and dec_seq % CHUNK != 0 and dec_seq & (dec_seq - 1) == 0
    assert (batch * CHUNK) % TM_TOK == 0 and n_seq * dec_seq == TM_TOK and n_seq % SAMPLE_NB == 0
    n_chunks = seq // CHUNK
    n_prompt = batch * seq

    xp = x_prompt.reshape(batch, n_chunks, CHUNK, D_MODEL).transpose(1, 0, 2, 3).reshape(n_prompt, D_MODEL)
    x = jnp.concatenate([xp, x_sample.reshape(n_seq * dec_seq, D_MODEL)], axis=0)

    row3 = lambda a: a.reshape(a.shape[0], 1, a.shape[-1])
    w_all = _regroup_w_in(w_in)
    wlr_pad = jnp.pad(gla_w_gate_lr, ((0, 0), (0, SMALL_W - GLA_RANK), (0, 0))).astype(BF16)
    ml_bias = jnp.pad(jnp.concatenate([mlstm_b_i, mlstm_b_f], axis=-1),
                      ((0, 0), (SM_I, SMALL_W - SM_I - 2 * HEADS)))
    scan_params = (hgrn_lb_raw, row3(hgrn_out_norm), wlr_pad, row3(gla_b_gate), row3(gla_out_norm),
                   row3(ml_bias), row3(mlstm_out_norm))
    dmat_p = jnp.asarray(_decay_matrix(CHUNK), BF16)
    dmat_s = jnp.asarray(_decay_matrix(dec_seq), BF16)
    sample_states = (state_hgrn, state_gla, state_mlstm_C,
                     state_mlstm_n.reshape(depth, n_seq, HEADS, 1, ML_DK),
                     jnp.broadcast_to(state_mlstm_m[..., None, None], (depth, n_seq, HEADS, 1, SMALL_W)))
    ffn_w = [(row3(ffn1_norm), ffn1_w_up.astype(BF16), ffn1_w_down.astype(BF16)),
             (row3(ffn2_norm), ffn2_w_up.astype(BF16), ffn2_w_down.astype(BF16))]
    w_branch_b, w_out_b, mix_norm3 = w_branch.astype(BF16), w_out.astype(BF16), row3(mix_norm)
    fin = final_norm.reshape(1, D_MODEL)

    p_states, s_states = [], []
    for l in range(depth):
        x = _ffn(x, *ffn_w[0], fin, l, False)
        p_all = _inproj(x, mix_norm3, w_all, l)
        o_p, *ps = _prompt_scan(p_all, scan_params, dmat_p, l, depth, batch, n_chunks)
        p_s = p_all[n_prompt:].reshape(n_seq, dec_seq, P_COLS)
        o_s, *ss = _sample_scan(p_s, scan_params, dmat_s, sample_states, l, depth)
        x = _merge(x, o_p, o_s.reshape(n_seq * dec_seq, N_BRANCH * BRANCH_W), p_all, w_branch_b, w_out_b, l)
        x = _ffn(x, *ffn_w[1], fin, l, l == depth - 1)
        p_states.append(ps)
        s_states.append(ss)

    y_prompt = x[:n_prompt].reshape(n_chunks, batch, CHUNK, D_MODEL).transpose(1, 0, 2, 3).reshape(batch, seq, D_MODEL)
    y_sample = x[n_prompt:].reshape(n_seq, dec_seq, D_MODEL)

    def stacked(states):
        hg, gla, mc, mn, mm = (jnp.stack([st[i] for st in states]) for i in range(len(STATE_TAILS)))
        return hg, gla, mc, mn[..., 0, :], mm[..., 0, 0]

    return (y_prompt, y_sample) + stacked(p_states) + stacked(s_states)
```

```python
import functools

import numpy as np
import jax
import jax.numpy as jnp
from jax import lax
from jax.experimental import pallas as pl
from jax.experimental.pallas import tpu as pltpu

D_MODEL = 1024
HEADS = 4
HG_DK = 128
GLA_DK = 64
GLA_RANK = 16
GLA_GATE_NORM = 16.0
ML_DK = 128
DV = 128
BRANCH_W = 512
N_BRANCH = 3
D_FF = 2816
CHUNK = 64
EPS = 1e-6
NEG_BIG = -1e30

F32 = jnp.float32
BF16 = jnp.bfloat16

COL_HG = 0
COL_ML = 2048
COL_GATE = 4096
COL_GLA_QK = 7168
COL_GLA_V = 7680
COL_GLA_G = 8192
COL_SMALL = 8704
P_COLS = 8832
SMALL_W = 128
SM_I = GLA_RANK
SM_F = GLA_RANK + HEADS

TM_FFN = 256
TM_TOK = 512
PROJ_COL_TILE = 2944
SAMPLE_NB = 8
VMEM_LIMIT = 56 * 1024 * 1024


def _cparams(sem):
    return pltpu.CompilerParams(dimension_semantics=sem, vmem_limit_bytes=VMEM_LIMIT)


def _dot(a, b):
    return jnp.dot(a, b, preferred_element_type=F32)


def _dot_nt(a, b):
    return lax.dot_general(a, b, (((1,), (1,)), ((), ())), preferred_element_type=F32)


def _dot_tn(a, b):
    return lax.dot_general(a, b, (((0,), (0,)), ((), ())), preferred_element_type=F32)


def _rms(x, g):
    return x * lax.rsqrt(jnp.mean(x * x, axis=-1, keepdims=True) + EPS) * g


def _log_sigmoid(x):
    return jnp.minimum(x, 0.0) - jnp.log1p(jnp.exp(-jnp.abs(x)))


def _silu(x):
    return x * jax.nn.sigmoid(x)


def _exact_dot(m_bf16, x):
    hi = x.astype(BF16)
    r1 = x - hi.astype(F32)
    mid = r1.astype(BF16)
    lo = (r1 - mid.astype(F32)).astype(BF16)
    return _dot(m_bf16, hi) + _dot(m_bf16, mid) + _dot(m_bf16, lo)


def _level_sizes(c):
    out, m = [], c // 2
    while m >= 1:
        out.append(m)
        m //= 2
    return out


def _decay_matrix(c):
    blocks = []
    for m in _level_sizes(c):
        mat = np.zeros((c, c), np.float32)
        for t in range(c):
            mid = (t // (2 * m)) * (2 * m) + m
            if t >= mid:
                mat[t, mid:t + 1] = 1.0
            else:
                mat[t, t + 1:mid] = 1.0
        blocks.append(mat)
    blocks.append(np.tril(np.ones((c, c), np.float32)))
    blocks.append(np.triu(np.ones((c, c), np.float32), 1))
    return np.concatenate(blocks, axis=0)


def _eye(n):
    return lax.broadcasted_iota(jnp.int32, (n, n), 0) == lax.broadcasted_iota(jnp.int32, (n, n), 1)


def _pair_masks(c):
    ti = lax.broadcasted_iota(jnp.int32, (c, c), 0)
    si = lax.broadcasted_iota(jnp.int32, (c, c), 1)
    levels = []
    for m in _level_sizes(c):
        same = (ti // (2 * m)) == (si // (2 * m))
        levels.append(same & ((ti & m) != 0) & ((si & m) == 0))
    return levels, ti == si, si <= ti


def _column_of(row, eye):
    return jnp.sum(jnp.where(eye, row, 0.0), axis=1, keepdims=True)


def _row_of(col, eye):
    return jnp.sum(jnp.where(eye, col, 0.0), axis=0, keepdims=True)


def _ffn_body(x_ref, g_ref, wup_ref, wdn_ref, fin_ref, o_ref, *, final):
    x = x_ref[...]
    h = _rms(x, g_ref[...]).astype(BF16)
    gu = _dot(h, wup_ref[...])
    act = _silu(gu[:, :D_FF]) * gu[:, D_FF:]
    out = x + 0.5 * _dot(act.astype(BF16), wdn_ref[...])
    if final:
        out = _rms(out, fin_ref[...])
    o_ref[...] = out


def _ffn(x, norm, w_up, w_down, final_norm, layer, final):
    t = x.shape[0]
    return pl.pallas_call(
        functools.partial(_ffn_body, final=final),
        grid=(t // TM_FFN,),
        in_specs=[
            pl.BlockSpec((TM_FFN, D_MODEL), lambda i: (i, 0)),
            pl.BlockSpec((None, 1, D_MODEL), lambda i: (layer, 0, 0)),
            pl.BlockSpec((None, D_MODEL, 2 * D_FF), lambda i: (layer, 0, 0)),
            pl.BlockSpec((None, D_FF, D_MODEL), lambda i: (layer, 0, 0)),
            pl.BlockSpec((1, D_MODEL), lambda i: (0, 0)),
        ],
        out_specs=pl.BlockSpec((TM_FFN, D_MODEL), lambda i: (i, 0)),
        out_shape=jax.ShapeDtypeStruct((t, D_MODEL), F32),
        compiler_params=_cparams(("arbitrary",)),
        name="ffn",
    )(x, norm, w_up, w_down, final_norm)


def _inproj_body(x_ref, g_ref, w_ref, o_ref):
    h = _rms(x_ref[...], g_ref[...]).astype(BF16)
    o_ref[...] = _dot(h, w_ref[...])


def _inproj(x, norm, w_all, layer):
    t = x.shape[0]
    return pl.pallas_call(
        _inproj_body,
        grid=(P_COLS // PROJ_COL_TILE, t // TM_TOK),
        in_specs=[
            pl.BlockSpec((TM_TOK, D_MODEL), lambda j, i: (i, 0)),
            pl.BlockSpec((None, 1, D_MODEL), lambda j, i: (layer, 0, 0)),
            pl.BlockSpec((None, D_MODEL, PROJ_COL_TILE), lambda j, i: (layer, 0, j)),
        ],
        out_specs=pl.BlockSpec((TM_TOK, PROJ_COL_TILE), lambda j, i: (i, j)),
        out_shape=jax.ShapeDtypeStruct((t, P_COLS), F32),
        compiler_params=_cparams(("arbitrary", "arbitrary")),
        name="inproj",
    )(x, norm, w_all)


def _merge_body(x_ref, op_ref, os_ref, g0_ref, g1_ref, g2_ref, wb_ref, wo_ref, o_ref, *, n_prompt_tiles):
    is_prompt = pl.program_id(0) < n_prompt_tiles
    merged = None
    for c, g_ref in enumerate((g0_ref, g1_ref, g2_ref)):
        cs = slice(c * BRANCH_W, (c + 1) * BRANCH_W)
        br = jnp.where(is_prompt, op_ref[:, cs], os_ref[:, cs]).astype(BF16)
        term = jax.nn.sigmoid(g_ref[...]) * _dot(br, wb_ref[c])
        merged = term if merged is None else merged + term
    o_ref[...] = x_ref[...] + _dot(merged.astype(BF16), wo_ref[...])


def _merge(x, o_prompt, o_sample, p_all, w_branch, w_out, layer):
    t = x.shape[0]
    n_prompt_tiles = o_prompt.shape[0] // TM_TOK
    gate_blk = COL_GATE // D_MODEL

    def gate_spec(c):
        return pl.BlockSpec((TM_TOK, D_MODEL), lambda i: (i, gate_blk + c))

    return pl.pallas_call(
        functools.partial(_merge_body, n_prompt_tiles=n_prompt_tiles),
        grid=(t // TM_TOK,),
        in_specs=[
            pl.BlockSpec((TM_TOK, D_MODEL), lambda i: (i, 0)),
            pl.BlockSpec((TM_TOK, N_BRANCH * BRANCH_W), lambda i: (jnp.minimum(i, n_prompt_tiles - 1), 0)),
            pl.BlockSpec((TM_TOK, N_BRANCH * BRANCH_W), lambda i: (0, 0)),
            gate_spec(0), gate_spec(1), gate_spec(2),
            pl.BlockSpec((None, N_BRANCH, BRANCH_W, D_MODEL), lambda i: (layer, 0, 0, 0)),
            pl.BlockSpec((None, D_MODEL, D_MODEL), lambda i: (layer, 0, 0)),
        ],
        out_specs=pl.BlockSpec((TM_TOK, D_MODEL), lambda i: (i, 0)),
        out_shape=jax.ShapeDtypeStruct((t, D_MODEL), F32),
        compiler_params=_cparams(("arbitrary",)),
        name="merge",
    )(x, o_prompt, o_sample, p_all, p_all, p_all, w_branch, w_out)


def _layer_lower_bound(lb_raw, layer):
    e = jnp.exp(lb_raw - jnp.max(lb_raw, axis=0, keepdims=True))
    soft = e / jnp.sum(e, axis=0, keepdims=True)
    lb = jnp.zeros_like(soft[0:1])
    for j in range(1, layer + 1):
        lb = lb + soft[j:j + 1]
    return lb


def _head_norm(o, g):
    return o * lax.rsqrt(jnp.mean(o * o, axis=-1, keepdims=True) + EPS) * g


def _gated_linear_head(qh, kh, vh, pairs, cum, rev, st, eye_dk):
    c = qh.shape[0]
    att = None
    for aq, ak, mask in pairs:
        prod = _dot_nt((qh if aq is None else qh * aq).astype(BF16), (kh if ak is None else kh * ak).astype(BF16))
        term = jnp.where(mask, prod, 0.0)
        att = term if att is None else att + term
    o = _dot(att.astype(BF16), vh) + _dot((qh * jnp.exp(cum)).astype(BF16), st.astype(BF16))
    decay = _column_of(jnp.exp(cum[c - 1:c, :]), eye_dk)
    return o, st * decay + _dot_tn((kh * jnp.exp(rev)).astype(BF16), vh)


def _gated_linear_branch(q_of, k_of, v_of, logf_of, cum_of, dk, dmat, masks, read_state, write_state, emit, fast):
    level_masks, eye, tri = masks
    n_lev = len(level_masks)
    c = eye.shape[0]
    eye_dk = _eye(dk)
    if not fast:
        e_all = _exact_dot(dmat, logf_of())
    for h in range(HEADS):
        if fast:
            cum = cum_of(h)
            ref = cum[c // 2 - 1:c // 2, :]
            pairs = [(jnp.exp(cum - ref), jnp.exp(ref - cum), tri)]
            rev = cum[c - 1:c, :] - cum
        else:
            ks = slice(h * dk, (h + 1) * dk)
            pairs = [(None, None, eye)]
            for lv in range(n_lev):
                a = jnp.exp(e_all[lv * c:(lv + 1) * c, ks])
                pairs.append((a, a, level_masks[lv]))
            cum = e_all[n_lev * c:(n_lev + 1) * c, ks]
            rev = e_all[(n_lev + 1) * c:(n_lev + 2) * c, ks]
        o, s_new = _gated_linear_head(q_of(h), k_of(h), v_of(h).astype(BF16), pairs, cum, rev, read_state(h), eye_dk)
        write_state(h, s_new)
        emit(h, o)


def _mlstm_branch(q_of, k_of, v_of, gates, bcum, masks, read_state, write_state, emit):
    _, eye, tri = masks
    c = eye.shape[0]
    for h in range(HEADS):
        q_in, k_in, vh = q_of(h), k_of(h), v_of(h).astype(BF16)
        qb, kb = q_in.astype(BF16), k_in.astype(BF16)
        bcol = bcum[:, SM_F + h:SM_F + h + 1]
        icol = gates[:, SM_I + h:SM_I + h + 1]
        log_d = jnp.where(tri, bcol + _row_of(icol - bcol, eye), NEG_BIG)
        cst, nrow, m_prev = read_state(h)
        inter = bcol + m_prev
        m_t = jnp.maximum(inter, jnp.max(log_d, axis=1, keepdims=True))
        d = jnp.exp(log_d - m_t)
        w_inter = jnp.exp(inter - m_t)
        qk = _dot_nt(qb, kb) * d
        num = _dot(qk.astype(BF16), vh) + w_inter * _dot(qb, cst.astype(BF16))
        den = (jnp.sum(qk, axis=1, keepdims=True)
               + w_inter * jnp.sum(q_in.astype(F32) * nrow, axis=1, keepdims=True))
        emit(h, num / jnp.maximum(jnp.abs(den), jnp.exp(-m_t)))
        m_new = m_t[c - 1:c, :]
        b_end = bcol[c - 1:c, :]
        kw = jnp.exp(b_end - bcol + icol - m_new) * k_in.astype(F32)
        carry = jnp.exp(b_end + m_prev - m_new)
        write_state(h,
                    carry * cst + _dot_tn(kw.astype(BF16), vh),
                    carry * nrow + jnp.sum(kw, axis=0, keepdims=True),
                    m_new)


def _hgrn_log_decay(zf, lb):
    return jnp.log(lb + (1.0 - lb) * jax.nn.sigmoid(zf))


def _gla_log_decay(p_small, wlr, gla_b):
    return _log_sigmoid(_dot(p_small.astype(BF16), wlr) + gla_b) / GLA_GATE_NORM


def _mlstm_gates(p_small, ml_bias):
    lane = lax.broadcasted_iota(jnp.int32, p_small.shape, 1)
    biased = p_small + ml_bias
    return jnp.where(lane >= SM_F, _log_sigmoid(biased), biased)


def _finish(branch, o, norm_g, gate_pre):
    return _head_norm(o, norm_g) * (jax.nn.sigmoid(gate_pre) if branch == 2 else _silu(gate_pre))


def _gate_pre(load, branch, h):
    name, c0 = (("hg", 3 * HEADS * HG_DK), ("gg", 0), ("ml", 3 * HEADS * ML_DK))[branch]
    return load[name](c0 + h * DV, c0 + (h + 1) * DV)


def _gated_linear_sequence(load, cums, prm, dmat, c, hg_io, gla_io, emit, fast):
    lb, _, wlr, gla_b = prm[:4]
    masks = _pair_masks(c)
    w = HEADS * HG_DK
    head = lambda h, width=DV: (h * width, (h + 1) * width)
    hg, gqk = load["hg"], load["gqk"]

    _gated_linear_branch(
        lambda h: _silu(hg(*head(h))),
        lambda h: (1.0 - lb[:, slice(*head(h))]) * jax.nn.sigmoid(-hg(w + h * HG_DK, w + (h + 1) * HG_DK)),
        lambda h: hg(2 * w + h * DV, 2 * w + (h + 1) * DV),
        lambda: _hgrn_log_decay(hg(w, 2 * w), lb),
        lambda h: cums[0](*head(h, HG_DK)),
        HG_DK, dmat, masks, hg_io[0], hg_io[1], functools.partial(emit, 0), fast)

    wg = HEADS * GLA_DK
    _gated_linear_branch(
        lambda h: gqk(*head(h, GLA_DK)) * (GLA_DK ** -0.5),
        lambda h: gqk(wg + h * GLA_DK, wg + (h + 1) * GLA_DK),
        lambda h: load["gv"](*head(h)),
        lambda: _gla_log_decay(load["sm"](0, SMALL_W), wlr, gla_b),
        lambda h: cums[1](*head(h, GLA_DK)),
        GLA_DK, dmat, masks, gla_io[0], gla_io[1], functools.partial(emit, 1), fast)


def _sequence_chunk(load, cums, prm, dmat, c, state_io, emit, fast):
    hg_io, gla_io, ml_io = state_io
    ml_bias = prm[5]
    w = HEADS * ML_DK
    n_lev = len(_level_sizes(c))
    _gated_linear_sequence(load, cums, prm, dmat, c, hg_io, gla_io, emit, fast)
    gates = _mlstm_gates(load["sm"](0, SMALL_W), ml_bias)
    _mlstm_branch(
        lambda h: load["ml"](h * ML_DK, (h + 1) * ML_DK),
        lambda h: load["ml"](w + h * ML_DK, w + (h + 1) * ML_DK) * (ML_DK ** -0.5),
        lambda h: load["ml"](2 * w + h * DV, 2 * w + (h + 1) * DV),
        gates, _exact_dot(dmat[n_lev * c:(n_lev + 1) * c, :], gates), _pair_masks(c), ml_io[0], ml_io[1],
        functools.partial(emit, 2))


def _state_io(b, in_refs, out_refs):
    ihg, igla, imc, imn, imm = in_refs
    ohg, ogla, omc, omn, omm = out_refs

    def write_to(ref):
        def write(h, s):
            ref[b, h] = s
        return write

    def ml_write(h, c_new, n_new, m_new):
        omc[b, h] = c_new
        omn[b, h] = n_new
        omm[b, h] = jnp.broadcast_to(m_new, (1, SMALL_W))

    return ((lambda h: ihg[b, h], write_to(ohg)),
            (lambda h: igla[b, h], write_to(ogla)),
            (lambda h: (imc[b, h], imn[b, h], imm[b, h][:, 0:1]), ml_write))


BLOCK_NAMES = ("hg", "ml", "gqk", "gv", "gg", "sm")
N_BLOCKS = len(BLOCK_NAMES)
N_PARAMS = 7
STATE_TAILS = [(HG_DK, DV), (GLA_DK, DV), (ML_DK, DV), (1, ML_DK), (1, SMALL_W)]
SAFE_LOG_RANGE = 80.0


def _load_params(param_refs, layer):
    lb_ref, *rest = param_refs
    return (_layer_lower_bound(lb_ref[...], layer),) + tuple(r[...] for r in rest)


def _scan_tile(n_seq, c, take, put, block_refs, cum_refs, param_refs, dmat_ref, in_state_refs, out_state_refs,
               emit_for, layer):
    prm = _load_params(param_refs, layer)
    lb, _, wlr, gla_b = prm[:4]
    norms = (prm[1], prm[4], prm[6])
    dmat = dmat_ref[...]
    n_lev = len(_level_sizes(c))
    tri_bf16 = dmat[n_lev * c:(n_lev + 1) * c, :]
    refs = dict(zip(BLOCK_NAMES, block_refs))
    w = HEADS * HG_DK

    def cum_pass(b, worst):
        logs = (_hgrn_log_decay(take(b, refs["hg"], w, 2 * w), lb),
                _gla_log_decay(take(b, refs["sm"], 0, SMALL_W), wlr, gla_b))
        for ref, lg in zip(cum_refs, logs):
            cum = _exact_dot(tri_bf16, lg)
            put(b, ref, cum)
            dev = jnp.abs(cum - cum[c // 2 - 1:c // 2, :])
            worst = jnp.maximum(worst, jnp.max(jnp.max(dev, axis=1, keepdims=True), axis=0, keepdims=True))
        return worst

    worst = lax.fori_loop(0, n_seq, cum_pass, jnp.zeros((1, 1), F32))

    def run(fast):
        def per_sequence(b, carry):
            load = {name: functools.partial(take, b, ref) for name, ref in refs.items()}
            cums = [functools.partial(take, b, ref) for ref in cum_refs]
            store = emit_for(b)

            def emit(branch, h, o):
                store(branch, h, _finish(branch, o, norms[branch], _gate_pre(load, branch, h)))

            _sequence_chunk(load, cums, prm, dmat, c, _state_io(b, in_state_refs, out_state_refs), emit, fast)
            return carry

        def go():
            lax.fori_loop(0, n_seq, per_sequence, 0)
        return go

    lax.cond(worst[0, 0] < SAFE_LOG_RANGE, run(True), run(False))


def _param_specs(layer, depth):
    return [
        pl.BlockSpec((depth, HEADS * HG_DK), lambda *_: (0, 0)),
        pl.BlockSpec((None, 1, DV), lambda *_: (layer, 0, 0)),
        pl.BlockSpec((None, SMALL_W, HEADS * GLA_DK), lambda *_: (layer, 0, 0)),
        pl.BlockSpec((None, 1, HEADS * GLA_DK), lambda *_: (layer, 0, 0)),
        pl.BlockSpec((None, 1, DV), lambda *_: (layer, 0, 0)),
        pl.BlockSpec((None, 1, SMALL_W), lambda *_: (layer, 0, 0)),
        pl.BlockSpec((None, 1, DV), lambda *_: (layer, 0, 0)),
    ]


GL_SCRATCH = ("qs", "ks", "qe", "ke", "v", "dec")
ML_SCRATCH = ("q", "k", "v", "gates", "bcum")


def _chunk_cumsum(x, c):
    pos = lax.broadcasted_iota(jnp.int32, x.shape, 0) % c
    shift = 1
    while shift < c:
        x = x + jnp.where(pos >= shift, pltpu.roll(x, shift, 0), 0.0)
        shift *= 2
    return x


def _prepare_gated_linear(q, k, v, logf, n_seq, c, sc):
    n = logf.shape[1]
    cum = _chunk_cumsum(logf, c).reshape(n_seq, c, n)
    ref = cum[:, c // 2 - 1:c // 2, :]
    end = cum[:, c - 1:c, :]
    q3, k3 = q.reshape(n_seq, c, n), k.reshape(n_seq, c, n)
    flat = lambda x: x.reshape(n_seq * c, n).astype(BF16)
    sc["qs"][...] = flat(q3 * jnp.exp(cum - ref))
    sc["ks"][...] = flat(k3 * jnp.exp(ref - cum))
    sc["qe"][...] = flat(q3 * jnp.exp(cum))
    sc["ke"][...] = flat(k3 * jnp.exp(end - cum))
    sc["v"][...] = v.astype(BF16)
    decay = jnp.transpose(jnp.exp(cum[:, c - 1, :]))
    for b in range(n_seq):
        sc["dec"][b] = decay[:, b:b + 1]
    dev = jnp.abs(cum - ref).reshape(n_seq * c, n)
    return jnp.max(jnp.max(dev, axis=1, keepdims=True), axis=0, keepdims=True)


def _gated_linear_units(b, rows, dk, sc, st_ref, tri, emit):
    for h in range(HEADS):
        ks, vs = slice(h * dk, (h + 1) * dk), slice(h * DV, (h + 1) * DV)
        att = jnp.where(tri, _dot_nt(sc["qs"][rows, ks], sc["ks"][rows, ks]), 0.0).astype(BF16)
        vh = sc["v"][rows, vs]
        st = st_ref[b, h]
        emit(h, _dot(att, vh) + _dot(sc["qe"][rows, ks], st.astype(BF16)))
        st_ref[b, h] = st * sc["dec"][b, ks, :] + _dot_tn(sc["ke"][rows, ks], vh)


def _prompt_scan_body(*refs, layer, batch):
    refs = list(refs)
    take_n = lambda n: [refs.pop(0) for _ in range(n)]
    blocks = dict(zip(BLOCK_NAMES, take_n(N_BLOCKS)))
    param_refs = take_n(N_PARAMS)
    dmat_ref, o_ref = take_n(2)
    state_refs = take_n(len(STATE_TAILS))
    hg_sc = dict(zip(GL_SCRATCH, take_n(len(GL_SCRATCH))))
    gla_sc = dict(zip(GL_SCRATCH, take_n(len(GL_SCRATCH))))
    ml_sc = dict(zip(ML_SCRATCH, take_n(len(ML_SCRATCH))))
    c = CHUNK

    @pl.when(pl.program_id(0) == 0)
    def _():
        for r in state_refs:
            r[...] = jnp.zeros_like(r)

    prm = _load_params(param_refs, layer)
    lb, hg_norm, wlr, gla_b, gla_norm, ml_bias, ml_norm = prm
    tile = {name: (lambda c0, c1, ref=ref: ref[:, c0:c1]) for name, ref in blocks.items()}
    w, wg = HEADS * HG_DK, HEADS * GLA_DK
    hg, gqk, ml = tile["hg"], tile["gqk"], tile["ml"]

    zf = hg(w, 2 * w)
    worst = jnp.maximum(
        _prepare_gated_linear(_silu(hg(0, w)), (1.0 - lb) * jax.nn.sigmoid(-zf), hg(2 * w, 3 * w),
                              _hgrn_log_decay(zf, lb), batch, c, hg_sc),
        _prepare_gated_linear(gqk(0, wg) * (GLA_DK ** -0.5), gqk(wg, 2 * wg), tile["gv"](0, HEADS * DV),
                              _gla_log_decay(tile["sm"](0, SMALL_W), wlr, gla_b), batch, c, gla_sc))
    gates = _mlstm_gates(tile["sm"](0, SMALL_W), ml_bias)
    ml_sc["gates"][...] = gates
    ml_sc["bcum"][...] = _chunk_cumsum(gates, c)
    ml_sc["q"][...] = ml(0, w).astype(BF16)
    ml_sc["k"][...] = (ml(w, 2 * w) * (ML_DK ** -0.5)).astype(BF16)
    ml_sc["v"][...] = ml(2 * w, 3 * w).astype(BF16)

    rows = lambda b: pl.ds(pl.multiple_of(b * c, c), c)
    masks = _pair_masks(c)

    def emit_for(b):
        def emit(branch, h, val):
            c0 = branch * BRANCH_W + h * DV
            o_ref[rows(b), c0:c0 + DV] = val
        return emit

    def mlstm_sequence(b):
        r = rows(b)
        head = lambda name: (lambda h: ml_sc[name][r, h * ML_DK:(h + 1) * ML_DK])
        _mlstm_branch(head("q"), head("k"), head("v"), ml_sc["gates"][r, :], ml_sc["bcum"][r, :], masks,
                      *_state_io(b, state_refs, state_refs)[2], functools.partial(emit_for(b), 2))

    def single_reference():
        def per_sequence(b, carry):
            emit = emit_for(b)
            _gated_linear_units(b, rows(b), HG_DK, hg_sc, state_refs[0], masks[2], functools.partial(emit, 0))
            _gated_linear_units(b, rows(b), GLA_DK, gla_sc, state_refs[1], masks[2], functools.partial(emit, 1))
            mlstm_sequence(b)
            return carry

        lax.fori_loop(0, batch, per_sequence, 0)

    def level_split():
        dmat = dmat_ref[...]

        def per_sequence(b, carry):
            load = {name: (lambda c0, c1, ref=ref: ref[rows(b), c0:c1]) for name, ref in blocks.items()}
            hg_io, gla_io, _ = _state_io(b, state_refs, state_refs)
            _gated_linear_sequence(load, None, prm, dmat, c, hg_io, gla_io, emit_for(b), False)
            mlstm_sequence(b)
            return carry

        lax.fori_loop(0, batch, per_sequence, 0)

    lax.cond(worst[0, 0] < SAFE_LOG_RANGE, single_reference, level_split)

    norms = (hg_norm, gla_norm, ml_norm)
    for branch in range(N_BRANCH):
        for h in range(HEADS):
            cs = slice(branch * BRANCH_W + h * DV, branch * BRANCH_W + (h + 1) * DV)
            o_ref[:, cs] = _finish(branch, o_ref[:, cs], norms[branch], _gate_pre(tile, branch, h))


def _prompt_scan(p_all, params, dmat, layer, depth, batch, n_chunks):
    rows = batch * CHUNK

    def col(width, start):
        return pl.BlockSpec((rows, width), lambda c: (c, start // width))

    def gl_scratch(dk):
        return ([pltpu.VMEM((rows, HEADS * dk), BF16)] * 4 + [pltpu.VMEM((rows, HEADS * DV), BF16),
                                                              pltpu.VMEM((batch, HEADS * dk, 1), F32)])

    full = lambda shape: pl.BlockSpec(shape, lambda c: (0,) * len(shape))
    state_shapes = [(batch, HEADS) + t for t in STATE_TAILS]
    return pl.pallas_call(
        functools.partial(_prompt_scan_body, layer=layer, batch=batch),
        grid=(n_chunks,),
        in_specs=[col(2048, COL_HG), col(2048, COL_ML), col(512, COL_GLA_QK), col(512, COL_GLA_V),
                  col(512, COL_GLA_G), col(SMALL_W, COL_SMALL)] + _param_specs(layer, depth) + [full(dmat.shape)],
        out_specs=[pl.BlockSpec((rows, N_BRANCH * BRANCH_W), lambda c: (c, 0))] + [full(s) for s in state_shapes],
        out_shape=[jax.ShapeDtypeStruct((n_chunks * rows, N_BRANCH * BRANCH_W), F32)]
        + [jax.ShapeDtypeStruct(s, F32) for s in state_shapes],
        scratch_shapes=gl_scratch(HG_DK) + gl_scratch(GLA_DK)
        + [pltpu.VMEM((rows, HEADS * ML_DK), BF16)] * 3 + [pltpu.VMEM((rows, SMALL_W), F32)] * 2,
        compiler_params=_cparams(("arbitrary",)),
        name="prompt_scan",
    )(*([p_all] * N_BLOCKS), *params, dmat)


def _sample_scan_body(*refs, layer):
    block_refs = refs[:N_BLOCKS]
    param_refs = refs[N_BLOCKS:N_BLOCKS + N_PARAMS]
    dmat_ref = refs[N_BLOCKS + N_PARAMS]
    n_state = len(STATE_TAILS)
    first_state = N_BLOCKS + N_PARAMS + 1
    in_state_refs = refs[first_state:first_state + n_state]
    o_ref = refs[first_state + n_state]
    out_state_refs = refs[first_state + n_state + 1:first_state + 2 * n_state + 1]
    cum_refs = refs[first_state + 2 * n_state + 1:]

    def take(b, ref, c0, c1):
        return ref[b, :, c0:c1]

    def put(b, ref, val):
        ref[b] = val

    def emit_for(b):
        def emit(branch, h, val):
            c0 = branch * BRANCH_W + h * DV
            o_ref[b, :, c0:c0 + DV] = val
        return emit

    _scan_tile(o_ref.shape[0], o_ref.shape[1], take, put, block_refs, cum_refs, param_refs, dmat_ref,
               in_state_refs, out_state_refs, emit_for, layer)


def _sample_scan(p_s, params, dmat, states, layer, depth):
    n_seq, seq, _ = p_s.shape
    nb = SAMPLE_NB

    def col(width, start):
        return pl.BlockSpec((nb, seq, width), lambda i: (i, 0, start // width))

    full = lambda shape: pl.BlockSpec(shape, lambda i: (0,) * len(shape))
    return pl.pallas_call(
        functools.partial(_sample_scan_body, layer=layer),
        grid=(n_seq // nb,),
        in_specs=[col(2048, COL_HG), col(2048, COL_ML), col(512, COL_GLA_QK), col(512, COL_GLA_V),
                  col(512, COL_GLA_G), col(SMALL_W, COL_SMALL)] + _param_specs(layer, depth) + [full(dmat.shape)]
        + [pl.BlockSpec((None, nb, HEADS) + t, lambda i: (layer, i, 0, 0, 0)) for t in STATE_TAILS],
        out_specs=[pl.BlockSpec((nb, seq, N_BRANCH * BRANCH_W), lambda i: (i, 0, 0))]
        + [pl.BlockSpec((nb, HEADS) + t, lambda i: (i, 0, 0, 0)) for t in STATE_TAILS],
        out_shape=[jax.ShapeDtypeStruct((n_seq, seq, N_BRANCH * BRANCH_W), F32)]
        + [jax.ShapeDtypeStruct((n_seq, HEADS) + t, F32) for t in STATE_TAILS],
        scratch_shapes=[pltpu.VMEM((nb, seq, HEADS * HG_DK), F32), pltpu.VMEM((nb, seq, HEADS * GLA_DK), F32)],
        compiler_params=_cparams(("arbitrary",)),
        name="sample_scan",
    )(*([p_s] * N_BLOCKS), *params, dmat, *states)


def _regroup_w_in(w_in):
    hg, gla, lr, ml, gates_if, mg = (w_in[..., 0:2048], w_in[..., 2048:3584], w_in[..., 3584:3600],
                                     w_in[..., 3600:5648], w_in[..., 5648:5656], w_in[..., 5656:8728])
    pad = jnp.zeros(w_in.shape[:-1] + (SMALL_W - GLA_RANK - 2 * HEADS,), w_in.dtype)
    return jnp.concatenate([hg, ml, mg, gla, lr, gates_if, pad], axis=-1).astype(BF16)


def kernel(x_prompt, x_sample, state_hgrn, state_gla, state_mlstm_C, state_mlstm_n, state_mlstm_m,
           ffn1_norm, ffn1_w_up, ffn1_w_down, mix_norm, w_in, hgrn_lb_raw, hgrn_out_norm,
           gla_w_gate_lr, gla_b_gate, gla_out_norm, mlstm_b_i, mlstm_b_f, mlstm_out_norm,
           w_branch, w_out, ffn2_norm, ffn2_w_up, ffn2_w_down, final_norm):
    depth = w_in.shape[0]
    batch, seq, _ = x_prompt.shape
    n_seq, dec_seq, _ = x_sample.shape
    assert seq % CHUNK == 0 and dec_seq % CHUNK != 0 and dec_seq & (dec_seq - 1) == 0
    assert (batch * CHUNK) % TM_TOK == 0 and n_seq * dec_seq == TM_TOK and n_seq % SAMPLE_NB == 0
    n_chunks = seq // CHUNK
    n_prompt = batch * seq

    xp = x_prompt.reshape(batch, n_chunks, CHUNK, D_MODEL).transpose(1, 0, 2, 3).reshape(n_prompt, D_MODEL)
    x = jnp.concatenate([xp, x_sample.reshape(n_seq * dec_seq, D_MODEL)], axis=0)

    row3 = lambda a: a.reshape(a.shape[0], 1, a.shape[-1])
    w_all = _regroup_w_in(w_in)
    wlr_pad = jnp.pad(gla_w_gate_lr, ((0, 0), (0, SMALL_W - GLA_RANK), (0, 0))).astype(BF16)
    ml_bias = jnp.pad(jnp.concatenate([mlstm_b_i, mlstm_b_f], axis=-1),
                      ((0, 0), (SM_I, SMALL_W - SM_I - 2 * HEADS)))
    scan_params = (hgrn_lb_raw, row3(hgrn_out_norm), wlr_pad, row3(gla_b_gate), row3(gla_out_norm),
                   row3(ml_bias), row3(mlstm_out_norm))
    dmat_p = jnp.asarray(_decay_matrix(CHUNK), BF16)
    dmat_s = jnp.asarray(_decay_matrix(dec_seq), BF16)
    sample_states = (state_hgrn, state_gla, state_mlstm_C,
                     state_mlstm_n.reshape(depth, n_seq, HEADS, 1, ML_DK),
                     jnp.broadcast_to(state_mlstm_m[..., None, None], (depth, n_seq, HEADS, 1, SMALL_W)))
    ffn_w = [(row3(ffn1_norm), ffn1_w_up.astype(BF16), ffn1_w_down.astype(BF16)),
             (row3(ffn2_norm), ffn2_w_up.astype(BF16), ffn2_w_down.astype(BF16))]
    w_branch_b, w_out_b, mix_norm3 = w_branch.astype(BF16), w_out.astype(BF16), row3(mix_norm)
    fin = final_norm.reshape(1, D_MODEL)

    p_states, s_states = [], []
    for l in range(depth):
        x = _ffn(x, *ffn_w[0], fin, l, False)
        p_all = _inproj(x, mix_norm3, w_all, l)
        o_p, *ps = _prompt_scan(p_all, scan_params, dmat_p, l, depth, batch, n_chunks)
        p_s = p_all[n_prompt:].reshape(n_seq, dec_seq, P_COLS)
        o_s, *ss = _sample_scan(p_s, scan_params, dmat_s, sample_states, l, depth)
        x = _merge(x, o_p, o_s.reshape(n_seq * dec_seq, N_BRANCH * BRANCH_W), p_all, w_branch_b, w_out_b, l)
        x = _ffn(x, *ffn_w[1], fin, l, l == depth - 1)
        p_states.append(ps)
        s_states.append(ss)

    y_prompt = x[:n_prompt].reshape(n_chunks, batch, CHUNK, D_MODEL).transpose(1, 0, 2, 3).reshape(batch, seq, D_MODEL)
    y_sample = x[n_prompt:].reshape(n_seq, dec_seq, D_MODEL)

    def stacked(states):
        hg, gla, mc, mn, mm = (jnp.stack([st[i] for st in states]) for i in range(len(STATE_TAILS)))
        return hg, gla, mc, mn[..., 0, :], mm[..., 0, 0]

    return (y_prompt, y_sample) + stacked(p_states) + stacked(s_states)
```

```python
import functools

import numpy as np
import jax
import jax.numpy as jnp
from jax import lax
from jax.experimental import pallas as pl
from jax.experimental.pallas import tpu as pltpu

D_MODEL = 1024
HEADS = 4
HG_DK = 128
GLA_DK = 64
GLA_RANK = 16
GLA_GATE_NORM = 16.0
ML_DK = 128
DV = 128
BRANCH_W = 512
N_BRANCH = 3
D_FF = 2816
CHUNK = 64
EPS = 1e-6
NEG_BIG = -1e30

F32 = jnp.float32
BF16 = jnp.bfloat16

COL_HG = 0
COL_ML = 2048
COL_GATE = 4096
COL_GLA_QK = 7168
COL_GLA_V = 7680
COL_GLA_G = 8192
COL_SMALL = 8704
P_COLS = 8832
SMALL_W = 128
SM_I = GLA_RANK
SM_F = GLA_RANK + HEADS

TM_FFN = 256
TM_TOK = 512
PROJ_COL_TILE = 2944
SAMPLE_NB = 8
VMEM_LIMIT = 56 * 1024 * 1024


def _cparams(sem):
    return pltpu.CompilerParams(dimension_semantics=sem, vmem_limit_bytes=VMEM_LIMIT)


def _dot(a, b):
    return jnp.dot(a, b, preferred_element_type=F32)


def _dot_nt(a, b):
    return lax.dot_general(a, b, (((1,), (1,)), ((), ())), preferred_element_type=F32)


def _dot_tn(a, b):
    return lax.dot_general(a, b, (((0,), (0,)), ((), ())), preferred_element_type=F32)


def _rms(x, g):
    return x * lax.rsqrt(jnp.mean(x * x, axis=-1, keepdims=True) + EPS) * g


def _log_sigmoid(x):
    return jnp.minimum(x, 0.0) - jnp.log1p(jnp.exp(-jnp.abs(x)))


def _silu(x):
    return x * jax.nn.sigmoid(x)


def _exact_dot(m_bf16, x):
    hi = x.astype(BF16)
    r1 = x - hi.astype(F32)
    mid = r1.astype(BF16)
    lo = (r1 - mid.astype(F32)).astype(BF16)
    return _dot(m_bf16, hi) + _dot(m_bf16, mid) + _dot(m_bf16, lo)


def _level_sizes(c):
    out, m = [], c // 2
    while m >= 1:
        out.append(m)
        m //= 2
    return out


def _decay_matrix(c):
    blocks = []
    for m in _level_sizes(c):
        mat = np.zeros((c, c), np.float32)
        for t in range(c):
            mid = (t // (2 * m)) * (2 * m) + m
            if t >= mid:
                mat[t, mid:t + 1] = 1.0
            else:
                mat[t, t + 1:mid] = 1.0
        blocks.append(mat)
    blocks.append(np.tril(np.ones((c, c), np.float32)))
    blocks.append(np.triu(np.ones((c, c), np.float32), 1))
    return np.concatenate(blocks, axis=0)


def _eye(n):
    return lax.broadcasted_iota(jnp.int32, (n, n), 0) == lax.broadcasted_iota(jnp.int32, (n, n), 1)


def _pair_masks(c):
    ti = lax.broadcasted_iota(jnp.int32, (c, c), 0)
    si = lax.broadcasted_iota(jnp.int32, (c, c), 1)
    levels = []
    for m in _level_sizes(c):
        same = (ti // (2 * m)) == (si // (2 * m))
        levels.append(same & ((ti & m) != 0) & ((si & m) == 0))
    return levels, ti == si, si <= ti


def _column_of(row, eye):
    return jnp.sum(jnp.where(eye, row, 0.0), axis=1, keepdims=True)


def _row_of(col, eye):
    return jnp.sum(jnp.where(eye, col, 0.0), axis=0, keepdims=True)


def _ffn_body(x_ref, g_ref, wup_ref, wdn_ref, fin_ref, o_ref, *, final):
    x = x_ref[...]
    h = _rms(x, g_ref[...]).astype(BF16)
    gu = _dot(h, wup_ref[...])
    act = _silu(gu[:, :D_FF]) * gu[:, D_FF:]
    out = x + 0.5 * _dot(act.astype(BF16), wdn_ref[...])
    if final:
        out = _rms(out, fin_ref[...])
    o_ref[...] = out


def _ffn(x, norm, w_up, w_down, final_norm, layer, final):
    t = x.shape[0]
    return pl.pallas_call(
        functools.partial(_ffn_body, final=final),
        grid=(t // TM_FFN,),
        in_specs=[
            pl.BlockSpec((TM_FFN, D_MODEL), lambda i: (i, 0)),
            pl.BlockSpec((None, 1, D_MODEL), lambda i: (layer, 0, 0)),
            pl.BlockSpec((None, D_MODEL, 2 * D_FF), lambda i: (layer, 0, 0)),
            pl.BlockSpec((None, D_FF, D_MODEL), lambda i: (layer, 0, 0)),
            pl.BlockSpec((1, D_MODEL), lambda i: (0, 0)),
        ],
        out_specs=pl.BlockSpec((TM_FFN, D_MODEL), lambda i: (i, 0)),
        out_shape=jax.ShapeDtypeStruct((t, D_MODEL), F32),
        compiler_params=_cparams(("arbitrary",)),
        name="ffn",
    )(x, norm, w_up, w_down, final_norm)


def _inproj_body(x_ref, g_ref, w_ref, o_ref):
    h = _rms(x_ref[...], g_ref[...]).astype(BF16)
    o_ref[...] = _dot(h, w_ref[...])


def _inproj(x, norm, w_all, layer):
    t = x.shape[0]
    return pl.pallas_call(
        _inproj_body,
        grid=(P_COLS // PROJ_COL_TILE, t // TM_TOK),
        in_specs=[
            pl.BlockSpec((TM_TOK, D_MODEL), lambda j, i: (i, 0)),
            pl.BlockSpec((None, 1, D_MODEL), lambda j, i: (layer, 0, 0)),
            pl.BlockSpec((None, D_MODEL, PROJ_COL_TILE), lambda j, i: (layer, 0, j)),
        ],
        out_specs=pl.BlockSpec((TM_TOK, PROJ_COL_TILE), lambda j, i: (i, j)),
        out_shape=jax.ShapeDtypeStruct((t, P_COLS), F32),
        compiler_params=_cparams(("arbitrary", "arbitrary")),
        name="inproj",
    )(x, norm, w_all)


def _merge_body(x_ref, op_ref, os_ref, g0_ref, g1_ref, g2_ref, wb_ref, wo_ref, o_ref, *, n_prompt_tiles):
    is_prompt = pl.program_id(0) < n_prompt_tiles
    merged = None
    for c, g_ref in enumerate((g0_ref, g1_ref, g2_ref)):
        cs = slice(c * BRANCH_W, (c + 1) * BRANCH_W)
        br = jnp.where(is_prompt, op_ref[:, cs], os_ref[:, cs]).astype(BF16)
        term = jax.nn.sigmoid(g_ref[...]) * _dot(br, wb_ref[c])
        merged = term if merged is None else merged + term
    o_ref[...] = x_ref[...] + _dot(merged.astype(BF16), wo_ref[...])


def _merge(x, o_prompt, o_sample, p_all, w_branch, w_out, layer):
    t = x.shape[0]
    n_prompt_tiles = o_prompt.shape[0] // TM_TOK
    gate_blk = COL_GATE // D_MODEL

    def gate_spec(c):
        return pl.BlockSpec((TM_TOK, D_MODEL), lambda i: (i, gate_blk + c))

    return pl.pallas_call(
        functools.partial(_merge_body, n_prompt_tiles=n_prompt_tiles),
        grid=(t // TM_TOK,),
        in_specs=[
            pl.BlockSpec((TM_TOK, D_MODEL), lambda i: (i, 0)),
            pl.BlockSpec((TM_TOK, N_BRANCH * BRANCH_W), lambda i: (jnp.minimum(i, n_prompt_tiles - 1), 0)),
            pl.BlockSpec((TM_TOK, N_BRANCH * BRANCH_W), lambda i: (0, 0)),
            gate_spec(0), gate_spec(1), gate_spec(2),
            pl.BlockSpec((None, N_BRANCH, BRANCH_W, D_MODEL), lambda i: (layer, 0, 0, 0)),
            pl.BlockSpec((None, D_MODEL, D_MODEL), lambda i: (layer, 0, 0)),
        ],
        out_specs=pl.BlockSpec((TM_TOK, D_MODEL), lambda i: (i, 0)),
        out_shape=jax.ShapeDtypeStruct((t, D_MODEL), F32),
        compiler_params=_cparams(("arbitrary",)),
        name="merge",
    )(x, o_prompt, o_sample, p_all, p_all, p_all, w_branch, w_out)


def _gla_gate_body(sm_ref, wlr_ref, b_ref, o_ref):
    z = _dot(sm_ref[...].astype(BF16), wlr_ref[...]) + b_ref[...]
    o_ref[...] = _log_sigmoid(z) / GLA_GATE_NORM


def _gla_gate(p_all, wlr_pad, gla_b, layer):
    t = p_all.shape[0]
    n = HEADS * GLA_DK
    return pl.pallas_call(
        _gla_gate_body,
        grid=(t // TM_TOK,),
        in_specs=[
            pl.BlockSpec((TM_TOK, SMALL_W), lambda i: (i, COL_SMALL // SMALL_W)),
            pl.BlockSpec((None, SMALL_W, n), lambda i: (layer, 0, 0)),
            pl.BlockSpec((None, 1, n), lambda i: (layer, 0, 0)),
        ],
        out_specs=pl.BlockSpec((TM_TOK, n), lambda i: (i, 0)),
        out_shape=jax.ShapeDtypeStruct((t, n), F32),
        compiler_params=_cparams(("arbitrary",)),
        name="gla_gate",
    )(p_all, wlr_pad, gla_b)


def _layer_lower_bound(lb_raw, layer):
    e = jnp.exp(lb_raw - jnp.max(lb_raw, axis=0, keepdims=True))
    soft = e / jnp.sum(e, axis=0, keepdims=True)
    lb = jnp.zeros_like(soft[0:1])
    for j in range(1, layer + 1):
        lb = lb + soft[j:j + 1]
    return lb


def _head_norm(o, g):
    return o * lax.rsqrt(jnp.mean(o * o, axis=-1, keepdims=True) + EPS) * g


def _gated_linear_head(qh, kh, vh, pairs, cum, rev, st, eye_dk):
    c = qh.shape[0]
    att = None
    for aq, ak, mask in pairs:
        prod = _dot_nt((qh if aq is None else qh * aq).astype(BF16), (kh if ak is None else kh * ak).astype(BF16))
        term = jnp.where(mask, prod, 0.0)
        att = term if att is None else att + term
    o = _dot(att.astype(BF16), vh) + _dot((qh * jnp.exp(cum)).astype(BF16), st.astype(BF16))
    decay = _column_of(jnp.exp(cum[c - 1:c, :]), eye_dk)
    return o, st * decay + _dot_tn((kh * jnp.exp(rev)).astype(BF16), vh)


def _gated_linear_branch(q_of, k_of, v_of, logf_of, cum_of, dk, dmat, masks, read_state, write_state, emit, fast):
    level_masks, eye, tri = masks
    n_lev = len(level_masks)
    c = eye.shape[0]
    eye_dk = _eye(dk)
    if not fast:
        e_all = _exact_dot(dmat, logf_of())
    for h in range(HEADS):
        if fast:
            cum = cum_of(h)
            ref = cum[c // 2 - 1:c // 2, :]
            pairs = [(jnp.exp(cum - ref), jnp.exp(ref - cum), tri)]
            rev = cum[c - 1:c, :] - cum
        else:
            ks = slice(h * dk, (h + 1) * dk)
            pairs = [(None, None, eye)]
            for lv in range(n_lev):
                a = jnp.exp(e_all[lv * c:(lv + 1) * c, ks])
                pairs.append((a, a, level_masks[lv]))
            cum = e_all[n_lev * c:(n_lev + 1) * c, ks]
            rev = e_all[(n_lev + 1) * c:(n_lev + 2) * c, ks]
        o, s_new = _gated_linear_head(q_of(h), k_of(h), v_of(h).astype(BF16), pairs, cum, rev, read_state(h), eye_dk)
        write_state(h, s_new)
        emit(h, o)


def _mlstm_branch(q_of, k_of, v_of, gates, bcum, masks, read_state, write_state, emit):
    _, eye, tri = masks
    c = eye.shape[0]
    for h in range(HEADS):
        q_in, k_in, vh = q_of(h), k_of(h), v_of(h).astype(BF16)
        qb, kb = q_in.astype(BF16), k_in.astype(BF16)
        bcol = bcum[:, SM_F + h:SM_F + h + 1]
        icol = gates[:, SM_I + h:SM_I + h + 1]
        log_d = jnp.where(tri, bcol + _row_of(icol - bcol, eye), NEG_BIG)
        cst, nrow, m_prev = read_state(h)
        inter = bcol + m_prev
        m_t = jnp.maximum(inter, jnp.max(log_d, axis=1, keepdims=True))
        d = jnp.exp(log_d - m_t)
        w_inter = jnp.exp(inter - m_t)
        qk = _dot_nt(qb, kb) * d
        num = _dot(qk.astype(BF16), vh) + w_inter * _dot(qb, cst.astype(BF16))
        den = (jnp.sum(qk, axis=1, keepdims=True)
               + w_inter * jnp.sum(q_in.astype(F32) * nrow, axis=1, keepdims=True))
        emit(h, num / jnp.maximum(jnp.abs(den), jnp.exp(-m_t)))
        m_new = m_t[c - 1:c, :]
        b_end = bcol[c - 1:c, :]
        kw = jnp.exp(b_end - bcol + icol - m_new) * k_in.astype(F32)
        carry = jnp.exp(b_end + m_prev - m_new)
        write_state(h,
                    carry * cst + _dot_tn(kw.astype(BF16), vh),
                    carry * nrow + jnp.sum(kw, axis=0, keepdims=True),
                    m_new)


def _hgrn_log_decay(zf, lb):
    return jnp.log(lb + (1.0 - lb) * jax.nn.sigmoid(zf))


def _mlstm_gates(p_small, ml_bias):
    lane = lax.broadcasted_iota(jnp.int32, p_small.shape, 1)
    biased = p_small + ml_bias
    return jnp.where(lane >= SM_F, _log_sigmoid(biased), biased)


def _finish(branch, o, norm_g, gate_pre):
    return _head_norm(o, norm_g) * (jax.nn.sigmoid(gate_pre) if branch == 2 else _silu(gate_pre))


def _gate_pre(load, branch, h):
    name, c0 = (("hg", 3 * HEADS * HG_DK), ("gg", 0), ("ml", 3 * HEADS * ML_DK))[branch]
    return load[name](c0 + h * DV, c0 + (h + 1) * DV)


def _gated_linear_sequence(load, cums, prm, dmat, c, hg_io, gla_io, emit, fast):
    lb = prm["lb"]
    masks = _pair_masks(c)
    w = HEADS * HG_DK
    head = lambda h, width=DV: (h * width, (h + 1) * width)
    hg, gqk = load["hg"], load["gqk"]

    _gated_linear_branch(
        lambda h: _silu(hg(*head(h))),
        lambda h: (1.0 - lb[:, slice(*head(h))]) * jax.nn.sigmoid(-hg(w + h * HG_DK, w + (h + 1) * HG_DK)),
        lambda h: hg(2 * w + h * DV, 2 * w + (h + 1) * DV),
        lambda: _hgrn_log_decay(hg(w, 2 * w), lb),
        lambda h: cums[0](*head(h, HG_DK)),
        HG_DK, dmat, masks, hg_io[0], hg_io[1], functools.partial(emit, 0), fast)

    wg = HEADS * GLA_DK
    _gated_linear_branch(
        lambda h: gqk(*head(h, GLA_DK)) * (GLA_DK ** -0.5),
        lambda h: gqk(wg + h * GLA_DK, wg + (h + 1) * GLA_DK),
        lambda h: load["gv"](*head(h)),
        lambda: load["ga"](0, wg),
        lambda h: cums[1](*head(h, GLA_DK)),
        GLA_DK, dmat, masks, gla_io[0], gla_io[1], functools.partial(emit, 1), fast)


def _sequence_chunk(load, cums, prm, dmat, c, state_io, emit, fast):
    hg_io, gla_io, ml_io = state_io
    w = HEADS * ML_DK
    n_lev = len(_level_sizes(c))
    _gated_linear_sequence(load, cums, prm, dmat, c, hg_io, gla_io, emit, fast)
    gates = _mlstm_gates(load["sm"](0, SMALL_W), prm["ml_bias"])
    _mlstm_branch(
        lambda h: load["ml"](h * ML_DK, (h + 1) * ML_DK),
        lambda h: load["ml"](w + h * ML_DK, w + (h + 1) * ML_DK) * (ML_DK ** -0.5),
        lambda h: load["ml"](2 * w + h * DV, 2 * w + (h + 1) * DV),
        gates, _exact_dot(dmat[n_lev * c:(n_lev + 1) * c, :], gates), _pair_masks(c), ml_io[0], ml_io[1],
        functools.partial(emit, 2))


def _state_io(b, in_refs, out_refs):
    ihg, igla, imc, imn, imm = in_refs
    ohg, ogla, omc, omn, omm = out_refs

    def write_to(ref):
        def write(h, s):
            ref[b, h] = s
        return write

    def ml_write(h, c_new, n_new, m_new):
        omc[b, h] = c_new
        omn[b, h] = n_new
        omm[b, h] = jnp.broadcast_to(m_new, (1, SMALL_W))

    return ((lambda h: ihg[b, h], write_to(ohg)),
            (lambda h: igla[b, h], write_to(ogla)),
            (lambda h: (imc[b, h], imn[b, h], imm[b, h][:, 0:1]), ml_write))


BLOCK_NAMES = ("hg", "ml", "gqk", "gv", "gg", "sm", "ga")
N_BLOCKS = len(BLOCK_NAMES)
PARAM_NAMES = ("hg_norm", "gla_norm", "ml_bias", "ml_norm")
N_PARAMS = 1 + len(PARAM_NAMES)
ML_SAFE_RANGE = 40.0
STATE_TAILS = [(HG_DK, DV), (GLA_DK, DV), (ML_DK, DV), (1, ML_DK), (1, SMALL_W)]
SAFE_LOG_RANGE = 80.0


def _load_params(param_refs, layer):
    lb_ref, *rest = param_refs
    return dict(zip(PARAM_NAMES, (r[...] for r in rest)), lb=_layer_lower_bound(lb_ref[...], layer))


def _scan_tile(n_seq, c, take, put, block_refs, cum_refs, param_refs, dmat_ref, in_state_refs, out_state_refs,
               emit_for, layer):
    prm = _load_params(param_refs, layer)
    lb = prm["lb"]
    norms = (prm["hg_norm"], prm["gla_norm"], prm["ml_norm"])
    dmat = dmat_ref[...]
    n_lev = len(_level_sizes(c))
    tri_bf16 = dmat[n_lev * c:(n_lev + 1) * c, :]
    refs = dict(zip(BLOCK_NAMES, block_refs))
    w = HEADS * HG_DK

    def cum_pass(b, worst):
        logs = (_hgrn_log_decay(take(b, refs["hg"], w, 2 * w), lb), take(b, refs["ga"], 0, HEADS * GLA_DK))
        for ref, lg in zip(cum_refs, logs):
            cum = _exact_dot(tri_bf16, lg)
            put(b, ref, cum)
            dev = jnp.abs(cum - cum[c // 2 - 1:c // 2, :])
            worst = jnp.maximum(worst, jnp.max(jnp.max(dev, axis=1, keepdims=True), axis=0, keepdims=True))
        return worst

    worst = lax.fori_loop(0, n_seq, cum_pass, jnp.zeros((1, 1), F32))

    def run(fast):
        def per_sequence(b, carry):
            load = {name: functools.partial(take, b, ref) for name, ref in refs.items()}
            cums = [functools.partial(take, b, ref) for ref in cum_refs]
            store = emit_for(b)

            def emit(branch, h, o):
                store(branch, h, _finish(branch, o, norms[branch], _gate_pre(load, branch, h)))

            _sequence_chunk(load, cums, prm, dmat, c, _state_io(b, in_state_refs, out_state_refs), emit, fast)
            return carry

        def go():
            lax.fori_loop(0, n_seq, per_sequence, 0)
        return go

    lax.cond(worst[0, 0] < SAFE_LOG_RANGE, run(True), run(False))


def _param_specs(layer, depth):
    return [
        pl.BlockSpec((depth, HEADS * HG_DK), lambda *_: (0, 0)),
        pl.BlockSpec((None, 1, DV), lambda *_: (layer, 0, 0)),
        pl.BlockSpec((None, 1, DV), lambda *_: (layer, 0, 0)),
        pl.BlockSpec((None, 1, SMALL_W), lambda *_: (layer, 0, 0)),
        pl.BlockSpec((None, 1, DV), lambda *_: (layer, 0, 0)),
    ]


GL_SCRATCH = ("qs", "ks", "qe", "ke", "v", "dec")
ML_SCRATCH = ("qm", "km", "qw", "kw", "v", "emt", "carry")


def _chunk_scan(x, c, combine, fill):
    pos = lax.broadcasted_iota(jnp.int32, x.shape, 0) % c
    shift = 1
    while shift < c:
        x = combine(x, jnp.where(pos >= shift, pltpu.roll(x, shift, 0), fill))
        shift *= 2
    return x


def _chunk_cumsum(x, c):
    return _chunk_scan(x, c, jnp.add, 0.0)


def _prepare_mlstm(q, k, v, gates, m_state, n_seq, c, sc):
    lane = lax.broadcasted_iota(jnp.int32, gates.shape, 1)
    g = jnp.where((lane >= SM_I) & (lane < SM_F + HEADS), gates, 0.0)
    bcum = _chunk_cumsum(g, c)
    a = pltpu.roll(g, HEADS, 1) - bcum
    lane3 = lax.broadcasted_iota(jnp.int32, (n_seq, 1, SMALL_W), 2)
    m_prev = jnp.zeros((n_seq, 1, SMALL_W), F32)
    for h in range(HEADS):
        m_prev = jnp.where(lane3 == SM_F + h, m_state[:, h], m_prev)
    tile3 = lambda x: x.reshape(n_seq, c, SMALL_W)
    flat = lambda x: x.reshape(n_seq * c, SMALL_W)
    a3, b3 = tile3(a), tile3(bcum)
    big_m = jnp.maximum(tile3(_chunk_scan(a, c, jnp.maximum, NEG_BIG)), m_prev)
    ref, m_end = big_m[:, c // 2 - 1:c // 2, :], big_m[:, c - 1:c, :]
    scales = {"qm": flat(jnp.exp(ref - big_m)), "km": flat(jnp.exp(a3 - ref)),
              "qw": flat(jnp.exp(m_prev - big_m)), "kw": flat(jnp.exp(a3 - m_end))}
    sc["emt"][...] = flat(jnp.exp(-(b3 + big_m)))
    sc["carry"][...] = jnp.exp(m_prev - m_end)
    for h in range(HEADS):
        hs = slice(h * ML_DK, (h + 1) * ML_DK)
        qh, kh = q[:, hs], k[:, hs] * (ML_DK ** -0.5)
        for name, x in (("qm", qh), ("km", kh), ("qw", qh), ("kw", kh)):
            sc[name][:, hs] = (x * scales[name][:, SM_F + h:SM_F + h + 1]).astype(BF16)
    sc["v"][...] = v.astype(BF16)
    dev = jnp.where((lane >= SM_F) & (lane < SM_F + HEADS), flat(jnp.abs(big_m - ref)), 0.0)
    return (jnp.max(jnp.max(dev, axis=1, keepdims=True), axis=0, keepdims=True),
            b3[:, c - 1:c, :] + m_end)


def _mlstm_units(b, rows, sc, c_ref, n_ref, tri, emit):
    for h in range(HEADS):
        hs = slice(h * ML_DK, (h + 1) * ML_DK)
        col = slice(SM_F + h, SM_F + h + 1)
        att = jnp.where(tri, _dot_nt(sc["qm"][rows, hs], sc["km"][rows, hs]), 0.0)
        vh, qw, kw = sc["v"][rows, hs], sc["qw"][rows, hs], sc["kw"][rows, hs]
        cst, nrow = c_ref[b, h], n_ref[b, h]
        num = _dot(att.astype(BF16), vh) + _dot(qw, cst.astype(BF16))
        den = jnp.sum(att, axis=1, keepdims=True) + jnp.sum(qw.astype(F32) * nrow, axis=1, keepdims=True)
        emit(h, num / jnp.maximum(jnp.abs(den), sc["emt"][rows, col]))
        carry = sc["carry"][b][:, col]
        c_ref[b, h] = carry * cst + _dot_tn(kw, vh)
        n_ref[b, h] = carry * nrow + jnp.sum(kw.astype(F32), axis=0, keepdims=True)


def _prepare_gated_linear(q, k, v, logf, n_seq, c, sc):
    n = logf.shape[1]
    cum = _chunk_cumsum(logf, c).reshape(n_seq, c, n)
    ref = cum[:, c // 2 - 1:c // 2, :]
    end = cum[:, c - 1:c, :]
    q3, k3 = q.reshape(n_seq, c, n), k.reshape(n_seq, c, n)
    flat = lambda x: x.reshape(n_seq * c, n).astype(BF16)
    sc["qs"][...] = flat(q3 * jnp.exp(cum - ref))
    sc["ks"][...] = flat(k3 * jnp.exp(ref - cum))
    sc["qe"][...] = flat(q3 * jnp.exp(cum))
    sc["ke"][...] = flat(k3 * jnp.exp(end - cum))
    sc["v"][...] = v.astype(BF16)
    decay = jnp.transpose(jnp.exp(cum[:, c - 1, :]))
    for b in range(n_seq):
        sc["dec"][b] = decay[:, b:b + 1]
    dev = jnp.abs(cum - ref).reshape(n_seq * c, n)
    return jnp.max(jnp.max(dev, axis=1, keepdims=True), axis=0, keepdims=True)


def _gated_linear_units(b, rows, dk, sc, st_ref, tri, emit):
    for h in range(HEADS):
        ks, vs = slice(h * dk, (h + 1) * dk), slice(h * DV, (h + 1) * DV)
        att = jnp.where(tri, _dot_nt(sc["qs"][rows, ks], sc["ks"][rows, ks]), 0.0).astype(BF16)
        vh = sc["v"][rows, vs]
        st = st_ref[b, h]
        emit(h, _dot(att, vh) + _dot(sc["qe"][rows, ks], st.astype(BF16)))
        st_ref[b, h] = st * sc["dec"][b, ks, :] + _dot_tn(sc["ke"][rows, ks], vh)


def _prompt_scan_body(*refs, layer, batch):
    refs = list(refs)
    take_n = lambda n: [refs.pop(0) for _ in range(n)]
    blocks = dict(zip(BLOCK_NAMES, take_n(N_BLOCKS)))
    param_refs = take_n(N_PARAMS)
    dmat_ref, o_ref = take_n(2)
    state_refs = take_n(len(STATE_TAILS))
    hg_sc = dict(zip(GL_SCRATCH, take_n(len(GL_SCRATCH))))
    gla_sc = dict(zip(GL_SCRATCH, take_n(len(GL_SCRATCH))))
    ml_sc = dict(zip(ML_SCRATCH, take_n(len(ML_SCRATCH))))
    c = CHUNK

    @pl.when(pl.program_id(0) == 0)
    def _():
        for r in state_refs:
            r[...] = jnp.zeros_like(r)

    prm = _load_params(param_refs, layer)
    lb = prm["lb"]
    tile = {name: (lambda c0, c1, ref=ref: ref[:, c0:c1]) for name, ref in blocks.items()}
    w, wg = HEADS * HG_DK, HEADS * GLA_DK
    hg, gqk, ml = tile["hg"], tile["gqk"], tile["ml"]

    zf = hg(w, 2 * w)
    gl_spread = jnp.maximum(
        _prepare_gated_linear(_silu(hg(0, w)), (1.0 - lb) * jax.nn.sigmoid(-zf), hg(2 * w, 3 * w),
                              _hgrn_log_decay(zf, lb), batch, c, hg_sc),
        _prepare_gated_linear(gqk(0, wg) * (GLA_DK ** -0.5), gqk(wg, 2 * wg), tile["gv"](0, HEADS * DV),
                              tile["ga"](0, wg), batch, c, gla_sc))
    ml_spread, m_new = _prepare_mlstm(ml(0, w), ml(w, 2 * w), ml(2 * w, 3 * w),
                                      _mlstm_gates(tile["sm"](0, SMALL_W), prm["ml_bias"]),
                                      state_refs[4][...], batch, c, ml_sc)
    worst = jnp.maximum(gl_spread * (1.0 / SAFE_LOG_RANGE), ml_spread * (1.0 / ML_SAFE_RANGE))

    rows = lambda b: pl.ds(pl.multiple_of(b * c, c), c)
    tri = _pair_masks(c)[2]

    def emit_for(b):
        def emit(branch, h, val):
            c0 = branch * BRANCH_W + h * DV
            o_ref[rows(b), c0:c0 + DV] = val
        return emit

    def factored():
        def per_sequence(b, carry):
            emit = emit_for(b)
            _gated_linear_units(b, rows(b), HG_DK, hg_sc, state_refs[0], tri, functools.partial(emit, 0))
            _gated_linear_units(b, rows(b), GLA_DK, gla_sc, state_refs[1], tri, functools.partial(emit, 1))
            _mlstm_units(b, rows(b), ml_sc, state_refs[2], state_refs[3], tri, functools.partial(emit, 2))
            return carry

        lax.fori_loop(0, batch, per_sequence, 0)
        for h in range(HEADS):
            state_refs[4][:, h] = jnp.broadcast_to(m_new[:, :, SM_F + h:SM_F + h + 1], (batch, 1, SMALL_W))

    def level_split():
        dmat = dmat_ref[...]

        def per_sequence(b, carry):
            load = {name: (lambda c0, c1, ref=ref: ref[rows(b), c0:c1]) for name, ref in blocks.items()}
            _sequence_chunk(load, None, prm, dmat, c, _state_io(b, state_refs, state_refs), emit_for(b), False)
            return carry

        lax.fori_loop(0, batch, per_sequence, 0)

    lax.cond(worst[0, 0] < 1.0, factored, level_split)

    norms = (prm["hg_norm"], prm["gla_norm"], prm["ml_norm"])
    for branch in range(N_BRANCH):
        for h in range(HEADS):
            cs = slice(branch * BRANCH_W + h * DV, branch * BRANCH_W + (h + 1) * DV)
            o_ref[:, cs] = _finish(branch, o_ref[:, cs], norms[branch], _gate_pre(tile, branch, h))


def _prompt_scan(p_all, log_a, params, dmat, layer, depth, batch, n_chunks):
    rows = batch * CHUNK

    def col(width, start):
        return pl.BlockSpec((rows, width), lambda c: (c, start // width))

    def gl_scratch(dk):
        return ([pltpu.VMEM((rows, HEADS * dk), BF16)] * 4 + [pltpu.VMEM((rows, HEADS * DV), BF16),
                                                              pltpu.VMEM((batch, HEADS * dk, 1), F32)])

    full = lambda shape: pl.BlockSpec(shape, lambda c: (0,) * len(shape))
    state_shapes = [(batch, HEADS) + t for t in STATE_TAILS]
    return pl.pallas_call(
        functools.partial(_prompt_scan_body, layer=layer, batch=batch),
        grid=(n_chunks,),
        in_specs=[col(2048, COL_HG), col(2048, COL_ML), col(512, COL_GLA_QK), col(512, COL_GLA_V),
                  col(512, COL_GLA_G), col(SMALL_W, COL_SMALL), col(HEADS * GLA_DK, 0)]
        + _param_specs(layer, depth) + [full(dmat.shape)],
        out_specs=[pl.BlockSpec((rows, N_BRANCH * BRANCH_W), lambda c: (c, 0))] + [full(s) for s in state_shapes],
        out_shape=[jax.ShapeDtypeStruct((n_chunks * rows, N_BRANCH * BRANCH_W), F32)]
        + [jax.ShapeDtypeStruct(s, F32) for s in state_shapes],
        scratch_shapes=gl_scratch(HG_DK) + gl_scratch(GLA_DK)
        + [pltpu.VMEM((rows, HEADS * ML_DK), BF16)] * 5
        + [pltpu.VMEM((rows, SMALL_W), F32), pltpu.VMEM((batch, 1, SMALL_W), F32)],
        compiler_params=_cparams(("arbitrary",)),
        name="prompt_scan",
    )(*([p_all] * (N_BLOCKS - 1)), log_a, *params, dmat)


def _sample_scan_body(*refs, layer):
    block_refs = refs[:N_BLOCKS]
    param_refs = refs[N_BLOCKS:N_BLOCKS + N_PARAMS]
    dmat_ref = refs[N_BLOCKS + N_PARAMS]
    n_state = len(STATE_TAILS)
    first_state = N_BLOCKS + N_PARAMS + 1
    in_state_refs = refs[first_state:first_state + n_state]
    o_ref = refs[first_state + n_state]
    out_state_refs = refs[first_state + n_state + 1:first_state + 2 * n_state + 1]
    cum_refs = refs[first_state + 2 * n_state + 1:]

    def take(b, ref, c0, c1):
        return ref[b, :, c0:c1]

    def put(b, ref, val):
        ref[b] = val

    def emit_for(b):
        def emit(branch, h, val):
            c0 = branch * BRANCH_W + h * DV
            o_ref[b, :, c0:c0 + DV] = val
        return emit

    _scan_tile(o_ref.shape[0], o_ref.shape[1], take, put, block_refs, cum_refs, param_refs, dmat_ref,
               in_state_refs, out_state_refs, emit_for, layer)


def _sample_scan(p_s, log_a, params, dmat, states, layer, depth):
    n_seq, seq, _ = p_s.shape
    nb = SAMPLE_NB

    def col(width, start):
        return pl.BlockSpec((nb, seq, width), lambda i: (i, 0, start // width))

    full = lambda shape: pl.BlockSpec(shape, lambda i: (0,) * len(shape))
    return pl.pallas_call(
        functools.partial(_sample_scan_body, layer=layer),
        grid=(n_seq // nb,),
        in_specs=[col(2048, COL_HG), col(2048, COL_ML), col(512, COL_GLA_QK), col(512, COL_GLA_V),
                  col(512, COL_GLA_G), col(SMALL_W, COL_SMALL), col(HEADS * GLA_DK, 0)]
        + _param_specs(layer, depth) + [full(dmat.shape)]
        + [pl.BlockSpec((None, nb, HEADS) + t, lambda i: (layer, i, 0, 0, 0)) for t in STATE_TAILS],
        out_specs=[pl.BlockSpec((nb, seq, N_BRANCH * BRANCH_W), lambda i: (i, 0, 0))]
        + [pl.BlockSpec((nb, HEADS) + t, lambda i: (i, 0, 0, 0)) for t in STATE_TAILS],
        out_shape=[jax.ShapeDtypeStruct((n_seq, seq, N_BRANCH * BRANCH_W), F32)]
        + [jax.ShapeDtypeStruct((n_seq, HEADS) + t, F32) for t in STATE_TAILS],
        scratch_shapes=[pltpu.VMEM((nb, seq, HEADS * HG_DK), F32), pltpu.VMEM((nb, seq, HEADS * GLA_DK), F32)],
        compiler_params=_cparams(("arbitrary",)),
        name="sample_scan",
    )(*([p_s] * (N_BLOCKS - 1)), log_a, *params, dmat, *states)


def _regroup_w_in(w_in):
    hg, gla, lr, ml, gates_if, mg = (w_in[..., 0:2048], w_in[..., 2048:3584], w_in[..., 3584:3600],
                                     w_in[..., 3600:5648], w_in[..., 5648:5656], w_in[..., 5656:8728])
    pad = jnp.zeros(w_in.shape[:-1] + (SMALL_W - GLA_RANK - 2 * HEADS,), w_in.dtype)
    return jnp.concatenate([hg, ml, mg, gla, lr, gates_if, pad], axis=-1).astype(BF16)


def kernel(x_prompt, x_sample, state_hgrn, state_gla, state_mlstm_C, state_mlstm_n, state_mlstm_m,
           ffn1_norm, ffn1_w_up, ffn1_w_down, mix_norm, w_in, hgrn_lb_raw, hgrn_out_norm,
           gla_w_gate_lr, gla_b_gate, gla_out_norm, mlstm_b_i, mlstm_b_f, mlstm_out_norm,
           w_branch, w_out, ffn2_norm, ffn2_w_up, ffn2_w_down, final_norm):
    depth = w_in.shape[0]
    batch, seq, _ = x_prompt.shape
    n_seq, dec_seq, _ = x_sample.shape
    assert seq % CHUNK == 0 and dec_seq % CHUNK != 0 and dec_seq & (dec_seq - 1) == 0
    assert (batch * CHUNK) % TM_TOK == 0 and n_seq * dec_seq == TM_TOK and n_seq % SAMPLE_NB == 0
    n_chunks = seq // CHUNK
    n_prompt = batch * seq

    xp = x_prompt.reshape(batch, n_chunks, CHUNK, D_MODEL).transpose(1, 0, 2, 3).reshape(n_prompt, D_MODEL)
    x = jnp.concatenate([xp, x_sample.reshape(n_seq * dec_seq, D_MODEL)], axis=0)

    row3 = lambda a: a.reshape(a.shape[0], 1, a.shape[-1])
    w_all = _regroup_w_in(w_in)
    wlr_pad = jnp.pad(gla_w_gate_lr, ((0, 0), (0, SMALL_W - GLA_RANK), (0, 0))).astype(BF16)
    ml_bias = jnp.pad(jnp.concatenate([mlstm_b_i, mlstm_b_f], axis=-1),
                      ((0, 0), (SM_I, SMALL_W - SM_I - 2 * HEADS)))
    scan_params = (hgrn_lb_raw, row3(hgrn_out_norm), row3(gla_out_norm), row3(ml_bias), row3(mlstm_out_norm))
    gla_b3 = row3(gla_b_gate)
    dmat_p = jnp.asarray(_decay_matrix(CHUNK), BF16)
    dmat_s = jnp.asarray(_decay_matrix(dec_seq), BF16)
    sample_states = (state_hgrn, state_gla, state_mlstm_C,
                     state_mlstm_n.reshape(depth, n_seq, HEADS, 1, ML_DK),
                     jnp.broadcast_to(state_mlstm_m[..., None, None], (depth, n_seq, HEADS, 1, SMALL_W)))
    ffn_w = [(row3(ffn1_norm), ffn1_w_up.astype(BF16), ffn1_w_down.astype(BF16)),
             (row3(ffn2_norm), ffn2_w_up.astype(BF16), ffn2_w_down.astype(BF16))]
    w_branch_b, w_out_b, mix_norm3 = w_branch.astype(BF16), w_out.astype(BF16), row3(mix_norm)
    fin = final_norm.reshape(1, D_MODEL)

    p_states, s_states = [], []
    for l in range(depth):
        x = _ffn(x, *ffn_w[0], fin, l, False)
        p_all = _inproj(x, mix_norm3, w_all, l)
        log_a = _gla_gate(p_all, wlr_pad, gla_b3, l)
        o_p, *ps = _prompt_scan(p_all, log_a, scan_params, dmat_p, l, depth, batch, n_chunks)
        p_s = p_all[n_prompt:].reshape(n_seq, dec_seq, P_COLS)
        log_a_s = log_a[n_prompt:].reshape(n_seq, dec_seq, HEADS * GLA_DK)
        o_s, *ss = _sample_scan(p_s, log_a_s, scan_params, dmat_s, sample_states, l, depth)
        x = _merge(x, o_p, o_s.reshape(n_seq * dec_seq, N_BRANCH * BRANCH_W), p_all, w_branch_b, w_out_b, l)
        x = _ffn(x, *ffn_w[1], fin, l, l == depth - 1)
        p_states.append(ps)
        s_states.append(ss)

    y_prompt = x[:n_prompt].reshape(n_chunks, batch, CHUNK, D_MODEL).transpose(1, 0, 2, 3).reshape(batch, seq, D_MODEL)
    y_sample = x[n_prompt:].reshape(n_seq, dec_seq, D_MODEL)

    def stacked(states):
        hg, gla, mc, mn, mm = (jnp.stack([st[i] for st in states]) for i in range(len(STATE_TAILS)))
        return hg, gla, mc, mn[..., 0, :], mm[..., 0, 0]

    return (y_prompt, y_sample) + stacked(p_states) + stacked(s_states)
```

```python
import functools
from typing import NamedTuple

import numpy as np
import jax
import jax.numpy as jnp
from jax import lax
from jax.experimental import pallas as pl
from jax.experimental.pallas import tpu as pltpu

D_MODEL = 1024
HEADS = 4
HG_DK = 128
GLA_DK = 64
GLA_RANK = 16
GLA_GATE_NORM = 16.0
ML_DK = 128
DV = 128
BRANCH_W = 512
N_BRANCH = 3
D_FF = 2816
CHUNK = 64
EPS = 1e-6
NEG_BIG = -1e30

F32 = jnp.float32
BF16 = jnp.bfloat16

COL_HG = 0
COL_ML = 2048
COL_GATE = 4096
COL_GLA_QK = 7168
COL_GLA_V = 7680
COL_GLA_G = 8192
COL_SMALL = 8704
P_COLS = 8832
SMALL_W = 128
SM_I = GLA_RANK
SM_F = GLA_RANK + HEADS

TM_FFN = 256
TM_TOK = 512
PROJ_COL_TILE = 2944
SAMPLE_NB = 8
VMEM_LIMIT = 56 * 1024 * 1024


def _cparams(sem):
    return pltpu.CompilerParams(dimension_semantics=sem, vmem_limit_bytes=VMEM_LIMIT)


def _dot(a, b):
    return jnp.dot(a, b, preferred_element_type=F32)


def _dot_nt(a, b):
    return lax.dot_general(a, b, (((1,), (1,)), ((), ())), preferred_element_type=F32)


def _dot_tn(a, b):
    return lax.dot_general(a, b, (((0,), (0,)), ((), ())), preferred_element_type=F32)


def _rms(x, g):
    return x * lax.rsqrt(jnp.mean(x * x, axis=-1, keepdims=True) + EPS) * g


def _log_sigmoid(x):
    return jnp.minimum(x, 0.0) - jnp.log1p(jnp.exp(-jnp.abs(x)))


def _silu(x):
    return x * jax.nn.sigmoid(x)


def _exact_dot(m_bf16, x):
    hi = x.astype(BF16)
    r1 = x - hi.astype(F32)
    mid = r1.astype(BF16)
    lo = (r1 - mid.astype(F32)).astype(BF16)
    return _dot(m_bf16, hi) + _dot(m_bf16, mid) + _dot(m_bf16, lo)


def _level_sizes(c):
    out, m = [], c // 2
    while m >= 1:
        out.append(m)
        m //= 2
    return out


def _decay_matrix(c):
    blocks = []
    for m in _level_sizes(c):
        mat = np.zeros((c, c), np.float32)
        for t in range(c):
            mid = (t // (2 * m)) * (2 * m) + m
            if t >= mid:
                mat[t, mid:t + 1] = 1.0
            else:
                mat[t, t + 1:mid] = 1.0
        blocks.append(mat)
    blocks.append(np.tril(np.ones((c, c), np.float32)))
    blocks.append(np.triu(np.ones((c, c), np.float32), 1))
    return np.concatenate(blocks, axis=0)


def _eye(n):
    return lax.broadcasted_iota(jnp.int32, (n, n), 0) == lax.broadcasted_iota(jnp.int32, (n, n), 1)


def _pair_masks(c):
    ti = lax.broadcasted_iota(jnp.int32, (c, c), 0)
    si = lax.broadcasted_iota(jnp.int32, (c, c), 1)
    levels = []
    for m in _level_sizes(c):
        same = (ti // (2 * m)) == (si // (2 * m))
        levels.append(same & ((ti & m) != 0) & ((si & m) == 0))
    return levels, ti == si, si <= ti


def _column_of(row, eye):
    return jnp.sum(jnp.where(eye, row, 0.0), axis=1, keepdims=True)


def _row_of(col, eye):
    return jnp.sum(jnp.where(eye, col, 0.0), axis=0, keepdims=True)


def _ffn_body(*refs, final, n_prompt_tiles, split_in, split_out):
    refs = list(refs)
    x_refs = [refs.pop(0) for _ in range(2 if split_in else 1)]
    g_ref, wup_ref, wdn_ref, fin_ref, *o_refs = refs
    is_prompt = pl.program_id(0) < n_prompt_tiles
    if split_in:
        x = jnp.where(is_prompt, x_refs[0][...].reshape(TM_FFN, D_MODEL), x_refs[1][...])
    else:
        x = x_refs[0][...]
    h = _rms(x, g_ref[...]).astype(BF16)
    gu = _dot(h, wup_ref[...])
    act = _silu(gu[:, :D_FF]) * gu[:, D_FF:]
    out = x + 0.5 * _dot(act.astype(BF16), wdn_ref[...])
    if final:
        out = _rms(out, fin_ref[...])
    if split_out:
        @pl.when(is_prompt)
        def _():
            o_refs[0][...] = out.reshape(o_refs[0].shape)

        @pl.when(jnp.logical_not(is_prompt))
        def _():
            o_refs[1][...] = out
    else:
        o_refs[0][...] = out


def _ffn(x, norm, w_up, w_down, final_norm, layer, final, prompt_shape, split_in=False, split_out=False):
    batch, n_chunks = prompt_shape
    n_prompt_tiles = batch * n_chunks * CHUNK // TM_FFN
    seqs_per_tile = TM_FFN // CHUNK
    tiles_per_chunk = batch // seqs_per_tile

    def prompt_idx(i):
        j = jnp.minimum(i, n_prompt_tiles - 1)
        return (j % tiles_per_chunk, j // tiles_per_chunk, 0, 0)

    split_specs = [pl.BlockSpec((seqs_per_tile, None, CHUNK, D_MODEL), prompt_idx),
                   pl.BlockSpec((TM_FFN, D_MODEL), lambda i: (jnp.maximum(i - n_prompt_tiles, 0), 0))]
    joined_spec = pl.BlockSpec((TM_FFN, D_MODEL), lambda i: (i, 0))
    xs = tuple(x) if split_in else (x,)
    n_sample = xs[1].shape[0] if split_in else x.shape[0] - n_prompt_tiles * TM_FFN
    t = n_prompt_tiles * TM_FFN + n_sample
    split_shapes = [jax.ShapeDtypeStruct((batch, n_chunks, CHUNK, D_MODEL), F32),
                    jax.ShapeDtypeStruct((n_sample, D_MODEL), F32)]
    return pl.pallas_call(
        functools.partial(_ffn_body, final=final, n_prompt_tiles=n_prompt_tiles, split_in=split_in,
                          split_out=split_out),
        grid=(t // TM_FFN,),
        in_specs=(split_specs if split_in else [joined_spec]) + [
            pl.BlockSpec((None, 1, D_MODEL), lambda i: (layer, 0, 0)),
            pl.BlockSpec((None, D_MODEL, 2 * D_FF), lambda i: (layer, 0, 0)),
            pl.BlockSpec((None, D_FF, D_MODEL), lambda i: (layer, 0, 0)),
            pl.BlockSpec((1, D_MODEL), lambda i: (0, 0)),
        ],
        out_specs=split_specs if split_out else joined_spec,
        out_shape=split_shapes if split_out else jax.ShapeDtypeStruct((t, D_MODEL), F32),
        compiler_params=_cparams(("arbitrary",)),
        name="ffn",
    )(*xs, norm, w_up, w_down, final_norm)


def _inproj_body(x_ref, g_ref, w_ref, o_ref):
    h = _rms(x_ref[...], g_ref[...]).astype(BF16)
    o_ref[...] = _dot(h, w_ref[...])


def _inproj(x, norm, w_all, layer):
    t = x.shape[0]
    return pl.pallas_call(
        _inproj_body,
        grid=(P_COLS // PROJ_COL_TILE, t // TM_TOK),
        in_specs=[
            pl.BlockSpec((TM_TOK, D_MODEL), lambda j, i: (i, 0)),
            pl.BlockSpec((None, 1, D_MODEL), lambda j, i: (layer, 0, 0)),
            pl.BlockSpec((None, D_MODEL, PROJ_COL_TILE), lambda j, i: (layer, 0, j)),
        ],
        out_specs=pl.BlockSpec((TM_TOK, PROJ_COL_TILE), lambda j, i: (i, j)),
        out_shape=jax.ShapeDtypeStruct((t, P_COLS), F32),
        compiler_params=_cparams(("arbitrary", "arbitrary")),
        name="inproj",
    )(x, norm, w_all)


def _merge_body(x_ref, op_ref, os_ref, g0_ref, g1_ref, g2_ref, wb_ref, wo_ref, o_ref, *, n_prompt_tiles):
    is_prompt = pl.program_id(0) < n_prompt_tiles
    merged = None
    for c, g_ref in enumerate((g0_ref, g1_ref, g2_ref)):
        cs = slice(c * BRANCH_W, (c + 1) * BRANCH_W)
        br = jnp.where(is_prompt, op_ref[:, cs], os_ref[:, cs]).astype(BF16)
        term = jax.nn.sigmoid(g_ref[...]) * _dot(br, wb_ref[c])
        merged = term if merged is None else merged + term
    o_ref[...] = x_ref[...] + _dot(merged.astype(BF16), wo_ref[...])


def _merge(x, o_prompt, o_sample, p_all, w_branch, w_out, layer):
    t = x.shape[0]
    n_prompt_tiles = o_prompt.shape[0] // TM_TOK
    gate_blk = COL_GATE // D_MODEL

    def gate_spec(c):
        return pl.BlockSpec((TM_TOK, D_MODEL), lambda i: (i, gate_blk + c))

    return pl.pallas_call(
        functools.partial(_merge_body, n_prompt_tiles=n_prompt_tiles),
        grid=(t // TM_TOK,),
        in_specs=[
            pl.BlockSpec((TM_TOK, D_MODEL), lambda i: (i, 0)),
            pl.BlockSpec((TM_TOK, N_BRANCH * BRANCH_W), lambda i: (jnp.minimum(i, n_prompt_tiles - 1), 0)),
            pl.BlockSpec((TM_TOK, N_BRANCH * BRANCH_W), lambda i: (0, 0)),
            gate_spec(0), gate_spec(1), gate_spec(2),
            pl.BlockSpec((None, N_BRANCH, BRANCH_W, D_MODEL), lambda i: (layer, 0, 0, 0)),
            pl.BlockSpec((None, D_MODEL, D_MODEL), lambda i: (layer, 0, 0)),
        ],
        out_specs=pl.BlockSpec((TM_TOK, D_MODEL), lambda i: (i, 0)),
        out_shape=jax.ShapeDtypeStruct((t, D_MODEL), F32),
        compiler_params=_cparams(("arbitrary",)),
        name="merge",
    )(x, o_prompt, o_sample, p_all, p_all, p_all, w_branch, w_out)


def _gla_gate_body(sm_ref, wlr_ref, b_ref, o_ref):
    z = _dot(sm_ref[...].astype(BF16), wlr_ref[...]) + b_ref[...]
    o_ref[...] = _log_sigmoid(z) / GLA_GATE_NORM


def _gla_gate(p_all, wlr_pad, gla_b, layer):
    t = p_all.shape[0]
    n = HEADS * GLA_DK
    return pl.pallas_call(
        _gla_gate_body,
        grid=(t // TM_TOK,),
        in_specs=[
            pl.BlockSpec((TM_TOK, SMALL_W), lambda i: (i, COL_SMALL // SMALL_W)),
            pl.BlockSpec((None, SMALL_W, n), lambda i: (layer, 0, 0)),
            pl.BlockSpec((None, 1, n), lambda i: (layer, 0, 0)),
        ],
        out_specs=pl.BlockSpec((TM_TOK, n), lambda i: (i, 0)),
        out_shape=jax.ShapeDtypeStruct((t, n), F32),
        compiler_params=_cparams(("arbitrary",)),
        name="gla_gate",
    )(p_all, wlr_pad, gla_b)


def _layer_lower_bound(lb_raw, layer):
    e = jnp.exp(lb_raw - jnp.max(lb_raw, axis=0, keepdims=True))
    soft = e / jnp.sum(e, axis=0, keepdims=True)
    lb = jnp.zeros_like(soft[0:1])
    for j in range(1, layer + 1):
        lb = lb + soft[j:j + 1]
    return lb


def _head_norm(o, g):
    return o * lax.rsqrt(jnp.mean(o * o, axis=-1, keepdims=True) + EPS) * g


def _gated_linear_branch(q_of, k_of, v_of, logf, dk, dmat, masks, read_state, write_state, emit):
    level_masks, eye, _ = masks
    n_lev = len(level_masks)
    c = eye.shape[0]
    eye_dk = _eye(dk)
    e_all = _exact_dot(dmat, logf)
    for h in range(HEADS):
        ks = slice(h * dk, (h + 1) * dk)
        qh, kh, vh = q_of(h), k_of(h), v_of(h).astype(BF16)
        att = jnp.where(eye, _dot_nt(qh.astype(BF16), kh.astype(BF16)), 0.0)
        for lv in range(n_lev):
            a = jnp.exp(e_all[lv * c:(lv + 1) * c, ks])
            att = att + jnp.where(level_masks[lv], _dot_nt((qh * a).astype(BF16), (kh * a).astype(BF16)), 0.0)
        cum = e_all[n_lev * c:(n_lev + 1) * c, ks]
        rev = e_all[(n_lev + 1) * c:(n_lev + 2) * c, ks]
        st = read_state(h)
        emit(h, _dot(att.astype(BF16), vh) + _dot((qh * jnp.exp(cum)).astype(BF16), st.astype(BF16)))
        decay = _column_of(jnp.exp(cum[c - 1:c, :]), eye_dk)
        write_state(h, st * decay + _dot_tn((kh * jnp.exp(rev)).astype(BF16), vh))


def _mlstm_branch(q_of, k_of, v_of, gates, bcum, masks, read_state, write_state, emit):
    _, eye, tri = masks
    c = eye.shape[0]
    for h in range(HEADS):
        qh, kh, vh = q_of(h), k_of(h), v_of(h).astype(BF16)
        qb = qh.astype(BF16)
        bcol = bcum[:, SM_F + h:SM_F + h + 1]
        icol = gates[:, SM_I + h:SM_I + h + 1]
        log_d = jnp.where(tri, bcol + _row_of(icol - bcol, eye), NEG_BIG)
        cst, nrow, m_prev = read_state(h)
        inter = bcol + m_prev
        m_t = jnp.maximum(inter, jnp.max(log_d, axis=1, keepdims=True))
        d = jnp.exp(log_d - m_t)
        w_inter = jnp.exp(inter - m_t)
        qk = _dot_nt(qb, kh.astype(BF16)) * d
        num = _dot(qk.astype(BF16), vh) + w_inter * _dot(qb, cst.astype(BF16))
        den = jnp.sum(qk, axis=1, keepdims=True) + w_inter * jnp.sum(qh * nrow, axis=1, keepdims=True)
        emit(h, num / jnp.maximum(jnp.abs(den), jnp.exp(-m_t)))
        m_new = m_t[c - 1:c, :]
        b_end = bcol[c - 1:c, :]
        kw = jnp.exp(b_end - bcol + icol - m_new) * kh
        carry = jnp.exp(b_end + m_prev - m_new)
        write_state(h,
                    carry * cst + _dot_tn(kw.astype(BF16), vh),
                    carry * nrow + jnp.sum(kw, axis=0, keepdims=True),
                    m_new)


def _hgrn_log_decay(zf, lb):
    return jnp.log(lb + (1.0 - lb) * jax.nn.sigmoid(zf))


def _mlstm_gates(p_small, ml_bias):
    lane = lax.broadcasted_iota(jnp.int32, p_small.shape, p_small.ndim - 1)
    biased = p_small + ml_bias
    return jnp.where(lane >= SM_F, _log_sigmoid(biased), biased)


def _finish(branch, o, norm_g, gate_pre):
    return _head_norm(o, norm_g) * (jax.nn.sigmoid(gate_pre) if branch == 2 else _silu(gate_pre))


def _gate_pre(load, branch, h):
    name, c0 = (("hg", 3 * HEADS * HG_DK), ("gg", 0), ("ml", 3 * HEADS * ML_DK))[branch]
    return load[name](c0 + h * DV, c0 + (h + 1) * DV)


def _sequence_chunk(load, prm, dmat, c, state_io, emit):
    hg_io, gla_io, ml_io = state_io
    lb = prm["lb"]
    masks = _pair_masks(c)
    n_lev = len(masks[0])
    w, wg = HEADS * HG_DK, HEADS * GLA_DK
    head = lambda h, width=DV: (h * width, (h + 1) * width)
    hg, gqk, ml = load["hg"], load["gqk"], load["ml"]

    _gated_linear_branch(
        lambda h: _silu(hg(*head(h))),
        lambda h: (1.0 - lb[:, slice(*head(h))]) * jax.nn.sigmoid(-hg(w + h * HG_DK, w + (h + 1) * HG_DK)),
        lambda h: hg(2 * w + h * DV, 2 * w + (h + 1) * DV),
        _hgrn_log_decay(hg(w, 2 * w), lb),
        HG_DK, dmat, masks, hg_io[0], hg_io[1], functools.partial(emit, 0))

    _gated_linear_branch(
        lambda h: gqk(*head(h, GLA_DK)) * (GLA_DK ** -0.5),
        lambda h: gqk(wg + h * GLA_DK, wg + (h + 1) * GLA_DK),
        lambda h: load["gv"](*head(h)),
        load["ga"](0, wg),
        GLA_DK, dmat, masks, gla_io[0], gla_io[1], functools.partial(emit, 1))

    gates = _mlstm_gates(load["sm"](0, SMALL_W), prm["ml_bias"])
    _mlstm_branch(
        lambda h: ml(*head(h)),
        lambda h: ml(w + h * ML_DK, w + (h + 1) * ML_DK) * (ML_DK ** -0.5),
        lambda h: ml(2 * w + h * DV, 2 * w + (h + 1) * DV),
        gates, _exact_dot(dmat[n_lev * c:(n_lev + 1) * c, :], gates), masks, ml_io[0], ml_io[1],
        functools.partial(emit, 2))


def _state_io(b, in_refs, out_refs):
    ihg, igla, imc, imn, imm = in_refs
    ohg, ogla, omc, omn, omm = out_refs

    def write_to(ref):
        def write(h, s):
            ref[b, h] = s
        return write

    def ml_write(h, c_new, n_new, m_new):
        omc[b, h] = c_new
        omn[b, h] = n_new
        omm[b, h] = jnp.broadcast_to(m_new, (1, SMALL_W))

    return ((lambda h: ihg[b, h], write_to(ohg)),
            (lambda h: igla[b, h], write_to(ogla)),
            (lambda h: (imc[b, h], imn[b, h], imm[b, h][:, 0:1]), ml_write))


BLOCK_NAMES = ("hg", "ml", "gqk", "gv", "gg", "sm", "ga")
N_BLOCKS = len(BLOCK_NAMES)
PARAM_NAMES = ("hg_norm", "gla_norm", "ml_bias", "ml_norm")
N_PARAMS = 1 + len(PARAM_NAMES)
STATE_TAILS = [(HG_DK, DV), (GLA_DK, DV), (ML_DK, DV), (1, ML_DK), (1, SMALL_W)]
GL_SCRATCH = ("qs", "ks", "qe", "ke", "v", "dec")
ML_SCRATCH = ("qm", "km", "qw", "kw", "v", "emt", "carry")
SAFE_LOG_RANGE = 80.0
ML_SAFE_RANGE = 40.0


class _Tile(NamedTuple):
    n_seq: int
    c: int
    flat: bool

    def shape(self, width):
        return (self.n_seq * self.c, width) if self.flat else (self.n_seq, self.c, width)

    def load(self, ref, c0, c1):
        if self.flat:
            return ref[:, c0:c1].reshape(self.n_seq, self.c, c1 - c0)
        return ref[:, :, c0:c1]

    def store(self, ref, c0, c1, x):
        if self.flat:
            ref[:, c0:c1] = x.reshape(self.n_seq * self.c, c1 - c0).astype(ref.dtype)
        else:
            ref[:, :, c0:c1] = x.astype(ref.dtype)

    def rows(self, b):
        return pl.ds(pl.multiple_of(b * self.c, self.c), self.c)

    def seq(self, ref, b, c0, c1):
        return ref[self.rows(b), c0:c1] if self.flat else ref[b, :, c0:c1]

    def seq_store(self, ref, b, c0, c1, x):
        if self.flat:
            ref[self.rows(b), c0:c1] = x
        else:
            ref[b, :, c0:c1] = x

    def scan(self, x, combine, fill):
        n_seq, c, w = x.shape
        if self.flat:
            y = x.reshape(n_seq * c, w)
            pos = lax.broadcasted_iota(jnp.int32, y.shape, 0) % c
            shift = 1
            while shift < c:
                y = combine(y, jnp.where(pos >= shift, pltpu.roll(y, shift, 0), fill))
                shift *= 2
            return y.reshape(n_seq, c, w)
        t = lax.broadcasted_iota(jnp.int32, x.shape, 1)
        acc = jnp.full(x.shape, fill, x.dtype)
        for j in range(c):
            acc = combine(acc, jnp.where(t >= j, x[:, j:j + 1, :], fill))
        return acc


def _max_all(x):
    for axis in (2, 1, 0):
        x = jnp.max(x, axis=axis, keepdims=True)
    return x


def _prepare_gated_linear(tile, q, k, v, logf, sc):
    c, n = tile.c, logf.shape[-1]
    cum = tile.scan(logf, jnp.add, 0.0)
    ref, end = cum[:, c // 2 - 1:c // 2, :], cum[:, c - 1:c, :]
    tile.store(sc["qs"], 0, n, q * jnp.exp(cum - ref))
    tile.store(sc["ks"], 0, n, k * jnp.exp(ref - cum))
    tile.store(sc["qe"], 0, n, q * jnp.exp(cum))
    tile.store(sc["ke"], 0, n, k * jnp.exp(end - cum))
    tile.store(sc["v"], 0, v.shape[-1], v)
    decay = jnp.transpose(jnp.exp(cum[:, c - 1, :]))
    for b in range(tile.n_seq):
        sc["dec"][b] = decay[:, b:b + 1]
    return _max_all(jnp.abs(cum - ref))


def _prepare_mlstm(tile, q, k, v, gates, m_state, sc):
    c = tile.c
    lane = lax.broadcasted_iota(jnp.int32, gates.shape, 2)
    g = jnp.where((lane >= SM_I) & (lane < SM_F + HEADS), gates, 0.0)
    bcum = tile.scan(g, jnp.add, 0.0)
    a = pltpu.roll(g, HEADS, 2) - bcum
    lane1 = lax.broadcasted_iota(jnp.int32, (tile.n_seq, 1, SMALL_W), 2)
    m_prev = jnp.zeros((tile.n_seq, 1, SMALL_W), F32)
    for h in range(HEADS):
        m_prev = jnp.where(lane1 == SM_F + h, m_state[:, h], m_prev)
    big_m = jnp.maximum(tile.scan(a, jnp.maximum, NEG_BIG), m_prev)
    ref, m_end = big_m[:, c // 2 - 1:c // 2, :], big_m[:, c - 1:c, :]
    scales = {"qm": jnp.exp(ref - big_m), "km": jnp.exp(a - ref),
              "qw": jnp.exp(m_prev - big_m), "kw": jnp.exp(a - m_end)}
    tile.store(sc["emt"], 0, SMALL_W, jnp.exp(-(bcum + big_m)))
    sc["carry"][...] = jnp.exp(m_prev - m_end)
    for h in range(HEADS):
        h0, h1 = h * ML_DK, (h + 1) * ML_DK
        qh, kh = q[:, :, h0:h1], k[:, :, h0:h1] * (ML_DK ** -0.5)
        for name, x in (("qm", qh), ("km", kh), ("qw", qh), ("kw", kh)):
            tile.store(sc[name], h0, h1, x * scales[name][:, :, SM_F + h:SM_F + h + 1])
    tile.store(sc["v"], 0, v.shape[-1], v)
    spread = jnp.where((lane >= SM_F) & (lane < SM_F + HEADS), jnp.abs(big_m - ref), 0.0)
    return _max_all(spread), bcum[:, c - 1:c, :] + m_end


def _gated_linear_units(tile, b, dk, sc, st_in, st_out, tri, emit):
    for h in range(HEADS):
        k0, k1 = h * dk, (h + 1) * dk
        att = jnp.where(tri, _dot_nt(tile.seq(sc["qs"], b, k0, k1), tile.seq(sc["ks"], b, k0, k1)), 0.0)
        vh = tile.seq(sc["v"], b, h * DV, (h + 1) * DV)
        st = st_in[b, h]
        emit(h, _dot(att.astype(BF16), vh) + _dot(tile.seq(sc["qe"], b, k0, k1), st.astype(BF16)))
        st_out[b, h] = st * sc["dec"][b, k0:k1, :] + _dot_tn(tile.seq(sc["ke"], b, k0, k1), vh)


def _mlstm_units(tile, b, sc, c_in, n_in, c_out, n_out, tri, emit):
    for h in range(HEADS):
        h0, h1 = h * ML_DK, (h + 1) * ML_DK
        att = jnp.where(tri, _dot_nt(tile.seq(sc["qm"], b, h0, h1), tile.seq(sc["km"], b, h0, h1)), 0.0)
        vh, qw, kw = (tile.seq(sc[name], b, h0, h1) for name in ("v", "qw", "kw"))
        cst, nrow = c_in[b, h], n_in[b, h]
        num = _dot(att.astype(BF16), vh) + _dot(qw, cst.astype(BF16))
        den = jnp.sum(att, axis=1, keepdims=True) + jnp.sum(qw.astype(F32) * nrow, axis=1, keepdims=True)
        emit(h, num / jnp.maximum(jnp.abs(den), tile.seq(sc["emt"], b, SM_F + h, SM_F + h + 1)))
        carry = sc["carry"][b][:, SM_F + h:SM_F + h + 1]
        c_out[b, h] = carry * cst + _dot_tn(kw, vh)
        n_out[b, h] = carry * nrow + jnp.sum(kw.astype(F32), axis=0, keepdims=True)


def _scan_body(*refs, layer, tile, zero_init):
    refs = list(refs)
    take_n = lambda n: [refs.pop(0) for _ in range(n)]
    blocks = dict(zip(BLOCK_NAMES, take_n(N_BLOCKS)))
    param_refs = take_n(N_PARAMS)
    dmat_ref, = take_n(1)
    in_states = None if zero_init else take_n(len(STATE_TAILS))
    o_ref, = take_n(1)
    out_states = take_n(len(STATE_TAILS))
    hg_sc = dict(zip(GL_SCRATCH, take_n(len(GL_SCRATCH))))
    gla_sc = dict(zip(GL_SCRATCH, take_n(len(GL_SCRATCH))))
    ml_sc = dict(zip(ML_SCRATCH, take_n(len(ML_SCRATCH))))
    c, n_seq = tile.c, tile.n_seq

    if zero_init:
        in_states = out_states

        @pl.when(pl.program_id(0) == 0)
        def _():
            for r in out_states:
                r[...] = jnp.zeros_like(r)

    lb_ref, *rest = param_refs
    prm = dict(zip(PARAM_NAMES, (r[...] for r in rest)), lb=_layer_lower_bound(lb_ref[...], layer))
    lb = prm["lb"]
    col = {name: functools.partial(tile.load, ref) for name, ref in blocks.items()}
    w, wg = HEADS * HG_DK, HEADS * GLA_DK
    hg, gqk, ml = col["hg"], col["gqk"], col["ml"]

    zf = hg(w, 2 * w)
    gl_spread = jnp.maximum(
        _prepare_gated_linear(tile, _silu(hg(0, w)), (1.0 - lb) * jax.nn.sigmoid(-zf), hg(2 * w, 3 * w),
                              _hgrn_log_decay(zf, lb), hg_sc),
        _prepare_gated_linear(tile, gqk(0, wg) * (GLA_DK ** -0.5), gqk(wg, 2 * wg), col["gv"](0, HEADS * DV),
                              col["ga"](0, wg), gla_sc))
    ml_spread, m_new = _prepare_mlstm(tile, ml(0, w), ml(w, 2 * w), ml(2 * w, 3 * w),
                                      _mlstm_gates(col["sm"](0, SMALL_W), prm["ml_bias"]),
                                      in_states[4][...], ml_sc)
    worst = jnp.maximum(gl_spread * (1.0 / SAFE_LOG_RANGE), ml_spread * (1.0 / ML_SAFE_RANGE))
    tri = _pair_masks(c)[2]

    def emit_for(b):
        def emit(branch, h, val):
            c0 = branch * BRANCH_W + h * DV
            tile.seq_store(o_ref, b, c0, c0 + DV, val)
        return emit

    def factored():
        def per_sequence(b, carry):
            emit = emit_for(b)
            _gated_linear_units(tile, b, HG_DK, hg_sc, in_states[0], out_states[0], tri, functools.partial(emit, 0))
            _gated_linear_units(tile, b, GLA_DK, gla_sc, in_states[1], out_states[1], tri, functools.partial(emit, 1))
            _mlstm_units(tile, b, ml_sc, in_states[2], in_states[3], out_states[2], out_states[3], tri,
                         functools.partial(emit, 2))
            return carry

        lax.fori_loop(0, n_seq, per_sequence, 0)
        for h in range(HEADS):
            out_states[4][:, h] = jnp.broadcast_to(m_new[:, :, SM_F + h:SM_F + h + 1], (n_seq, 1, SMALL_W))

    def per_head():
        dmat = dmat_ref[...]

        def per_sequence(b, carry):
            load = {name: functools.partial(tile.seq, ref, b) for name, ref in blocks.items()}
            _sequence_chunk(load, prm, dmat, c, _state_io(b, in_states, out_states), emit_for(b))
            return carry

        lax.fori_loop(0, n_seq, per_sequence, 0)

    lax.cond(worst[0, 0, 0] < 1.0, factored, per_head)

    norms = (prm["hg_norm"], prm["gla_norm"], prm["ml_norm"])
    for branch in range(N_BRANCH):
        for h in range(HEADS):
            c0 = branch * BRANCH_W + h * DV
            tile.store(o_ref, c0, c0 + DV,
                       _finish(branch, tile.load(o_ref, c0, c0 + DV), norms[branch], _gate_pre(col, branch, h)))


def _scan_scratch(tile):
    def gated_linear(dk):
        return ([pltpu.VMEM(tile.shape(HEADS * dk), BF16)] * 4
                + [pltpu.VMEM(tile.shape(HEADS * DV), BF16), pltpu.VMEM((tile.n_seq, HEADS * dk, 1), F32)])

    return (gated_linear(HG_DK) + gated_linear(GLA_DK) + [pltpu.VMEM(tile.shape(HEADS * ML_DK), BF16)] * 5
            + [pltpu.VMEM(tile.shape(SMALL_W), F32), pltpu.VMEM((tile.n_seq, 1, SMALL_W), F32)])


def _param_specs(layer, depth):
    return [
        pl.BlockSpec((depth, HEADS * HG_DK), lambda *_: (0, 0)),
        pl.BlockSpec((None, 1, DV), lambda *_: (layer, 0, 0)),
        pl.BlockSpec((None, 1, DV), lambda *_: (layer, 0, 0)),
        pl.BlockSpec((None, 1, SMALL_W), lambda *_: (layer, 0, 0)),
        pl.BlockSpec((None, 1, DV), lambda *_: (layer, 0, 0)),
    ]


BLOCK_COLS = ((2048, COL_HG), (2048, COL_ML), (512, COL_GLA_QK), (512, COL_GLA_V), (512, COL_GLA_G),
              (SMALL_W, COL_SMALL), (HEADS * GLA_DK, 0))


def _prompt_scan(p_all, log_a, params, dmat, layer, depth, batch, n_chunks):
    tile = _Tile(batch, CHUNK, True)
    rows = batch * CHUNK
    full = lambda shape: pl.BlockSpec(shape, lambda c: (0,) * len(shape))
    state_shapes = [(batch, HEADS) + t for t in STATE_TAILS]
    return pl.pallas_call(
        functools.partial(_scan_body, layer=layer, tile=tile, zero_init=True),
        grid=(n_chunks,),
        in_specs=[pl.BlockSpec((rows, width), lambda c, blk=start // width: (c, blk)) for width, start in BLOCK_COLS]
        + _param_specs(layer, depth) + [full(dmat.shape)],
        out_specs=[pl.BlockSpec((rows, N_BRANCH * BRANCH_W), lambda c: (c, 0))] + [full(s) for s in state_shapes],
        out_shape=[jax.ShapeDtypeStruct((n_chunks * rows, N_BRANCH * BRANCH_W), F32)]
        + [jax.ShapeDtypeStruct(s, F32) for s in state_shapes],
        scratch_shapes=_scan_scratch(tile),
        compiler_params=_cparams(("arbitrary",)),
        name="prompt_scan",
    )(*([p_all] * (N_BLOCKS - 1)), log_a, *params, dmat)


def _sample_scan(p_s, log_a, params, dmat, states, layer, depth):
    n_seq, seq, _ = p_s.shape
    nb = SAMPLE_NB
    tile = _Tile(nb, seq, False)
    full = lambda shape: pl.BlockSpec(shape, lambda i: (0,) * len(shape))
    return pl.pallas_call(
        functools.partial(_scan_body, layer=layer, tile=tile, zero_init=False),
        grid=(n_seq // nb,),
        in_specs=[pl.BlockSpec((nb, seq, width), lambda i, blk=start // width: (i, 0, blk)) for width, start in BLOCK_COLS]
        + _param_specs(layer, depth) + [full(dmat.shape)]
        + [pl.BlockSpec((None, nb, HEADS) + t, lambda i: (layer, i, 0, 0, 0)) for t in STATE_TAILS],
        out_specs=[pl.BlockSpec((nb, seq, N_BRANCH * BRANCH_W), lambda i: (i, 0, 0))]
        + [pl.BlockSpec((nb, HEADS) + t, lambda i: (i, 0, 0, 0)) for t in STATE_TAILS],
        out_shape=[jax.ShapeDtypeStruct((n_seq, seq, N_BRANCH * BRANCH_W), F32)]
        + [jax.ShapeDtypeStruct((n_seq, HEADS) + t, F32) for t in STATE_TAILS],
        scratch_shapes=_scan_scratch(tile),
        compiler_params=_cparams(("arbitrary",)),
        name="sample_scan",
    )(*([p_s] * (N_BLOCKS - 1)), log_a, *params, dmat, *states)


def _regroup_w_in(w_in):
    hg, gla, lr, ml, gates_if, mg = (w_in[..., 0:2048], w_in[..., 2048:3584], w_in[..., 3584:3600],
                                     w_in[..., 3600:5648], w_in[..., 5648:5656], w_in[..., 5656:8728])
    pad = jnp.zeros(w_in.shape[:-1] + (SMALL_W - GLA_RANK - 2 * HEADS,), w_in.dtype)
    return jnp.concatenate([hg, ml, mg, gla, lr, gates_if, pad], axis=-1).astype(BF16)


def kernel(x_prompt, x_sample, state_hgrn, state_gla, state_mlstm_C, state_mlstm_n, state_mlstm_m,
           ffn1_norm, ffn1_w_up, ffn1_w_down, mix_norm, w_in, hgrn_lb_raw, hgrn_out_norm,
           gla_w_gate_lr, gla_b_gate, gla_out_norm, mlstm_b_i, mlstm_b_f, mlstm_out_norm,
           w_branch, w_out, ffn2_norm, ffn2_w_up, ffn2_w_down, final_norm):
    depth = w_in.shape[0]
    batch, seq, _ = x_prompt.shape
    n_seq, dec_seq, _ = x_sample.shape
    assert seq % CHUNK == 0 and dec_seq % CHUNK != 0 and dec_seq & (dec_seq - 1) == 0
    assert (batch * CHUNK) % TM_TOK == 0 and n_seq * dec_seq == TM_TOK and n_seq % SAMPLE_NB == 0
    n_chunks = seq // CHUNK
    n_prompt = batch * seq

    assert TM_FFN % CHUNK == 0 and batch % (TM_FFN // CHUNK) == 0 and (n_seq * dec_seq) % TM_FFN == 0
    x = (x_prompt.reshape(batch, n_chunks, CHUNK, D_MODEL), x_sample.reshape(n_seq * dec_seq, D_MODEL))

    row3 = lambda a: a.reshape(a.shape[0], 1, a.shape[-1])
    w_all = _regroup_w_in(w_in)
    wlr_pad = jnp.pad(gla_w_gate_lr, ((0, 0), (0, SMALL_W - GLA_RANK), (0, 0))).astype(BF16)
    ml_bias = jnp.pad(jnp.concatenate([mlstm_b_i, mlstm_b_f], axis=-1),
                      ((0, 0), (SM_I, SMALL_W - SM_I - 2 * HEADS)))
    scan_params = (hgrn_lb_raw, row3(hgrn_out_norm), row3(gla_out_norm), row3(ml_bias), row3(mlstm_out_norm))
    gla_b3 = row3(gla_b_gate)
    dmat_p = jnp.asarray(_decay_matrix(CHUNK), BF16)
    dmat_s = jnp.asarray(_decay_matrix(dec_seq), BF16)
    sample_states = (state_hgrn, state_gla, state_mlstm_C,
                     state_mlstm_n.reshape(depth, n_seq, HEADS, 1, ML_DK),
                     jnp.broadcast_to(state_mlstm_m[..., None, None], (depth, n_seq, HEADS, 1, SMALL_W)))
    ffn_w = [(row3(ffn1_norm), ffn1_w_up.astype(BF16), ffn1_w_down.astype(BF16)),
             (row3(ffn2_norm), ffn2_w_up.astype(BF16), ffn2_w_down.astype(BF16))]
    w_branch_b, w_out_b, mix_norm3 = w_branch.astype(BF16), w_out.astype(BF16), row3(mix_norm)
    fin = final_norm.reshape(1, D_MODEL)

    p_states, s_states = [], []
    for l in range(depth):
        x = _ffn(x, *ffn_w[0], fin, l, False, (batch, n_chunks), split_in=l == 0)
        p_all = _inproj(x, mix_norm3, w_all, l)
        log_a = _gla_gate(p_all, wlr_pad, gla_b3, l)
        o_p, *ps = _prompt_scan(p_all, log_a, scan_params, dmat_p, l, depth, batch, n_chunks)
        p_s = p_all[n_prompt:].reshape(n_seq, dec_seq, P_COLS)
        log_a_s = log_a[n_prompt:].reshape(n_seq, dec_seq, HEADS * GLA_DK)
        o_s, *ss = _sample_scan(p_s, log_a_s, scan_params, dmat_s, sample_states, l, depth)
        x = _merge(x, o_p, o_s.reshape(n_seq * dec_seq, N_BRANCH * BRANCH_W), p_all, w_branch_b, w_out_b, l)
        last = l == depth - 1
        x = _ffn(x, *ffn_w[1], fin, l, last, (batch, n_chunks), split_out=last)
        p_states.append(ps)
        s_states.append(ss)

    y_prompt = x[0].reshape(batch, seq, D_MODEL)
    y_sample = x[1].reshape(n_seq, dec_seq, D_MODEL)

    def stacked(states):
        hg, gla, mc, mn, mm = (jnp.stack([st[i] for st in states]) for i in range(len(STATE_TAILS)))
        return hg, gla, mc, mn[..., 0, :], mm[..., 0, 0]

    return (y_prompt, y_sample) + stacked(p_states) + stacked(s_states)
```

```python
import functools
from typing import NamedTuple

import numpy as np
import jax
import jax.numpy as jnp
from jax import lax
from jax.experimental import pallas as pl
from jax.experimental.pallas import tpu as pltpu

D_MODEL = 1024
HEADS = 4
HG_DK = 128
GLA_DK = 64
GLA_RANK = 16
GLA_GATE_NORM = 16.0
ML_DK = 128
DV = 128
BRANCH_W = 512
N_BRANCH = 3
D_FF = 2816
CHUNK = 64
EPS = 1e-6
NEG_BIG = -1e30

F32 = jnp.float32
BF16 = jnp.bfloat16

COL_HG = 0
COL_ML = 2048
COL_GATE = 4096
COL_GLA_QK = 7168
COL_GLA_V = 7680
COL_GLA_G = 8192
COL_SMALL = 8704
P_COLS = 8832
SMALL_W = 128
SM_I = GLA_RANK
SM_F = GLA_RANK + HEADS

TM_FFN = 512
TM_TOK = 512
PROJ_COL_TILE = 2944
SAMPLE_NB = 8
VMEM_LIMIT = 56 * 1024 * 1024


def _cparams(sem):
    return pltpu.CompilerParams(dimension_semantics=sem, vmem_limit_bytes=VMEM_LIMIT)


def _dot(a, b):
    return jnp.dot(a, b, preferred_element_type=F32)


def _dot_nt(a, b):
    return lax.dot_general(a, b, (((1,), (1,)), ((), ())), preferred_element_type=F32)


def _dot_tn(a, b):
    return lax.dot_general(a, b, (((0,), (0,)), ((), ())), preferred_element_type=F32)


def _rms(x, g):
    return x * lax.rsqrt(jnp.mean(x * x, axis=-1, keepdims=True) + EPS) * g


def _log_sigmoid(x):
    return jnp.minimum(x, 0.0) - jnp.log1p(jnp.exp(-jnp.abs(x)))


def _silu(x):
    return x * jax.nn.sigmoid(x)


def _exact_dot(m_bf16, x):
    hi = x.astype(BF16)
    r1 = x - hi.astype(F32)
    mid = r1.astype(BF16)
    lo = (r1 - mid.astype(F32)).astype(BF16)
    return _dot(m_bf16, hi) + _dot(m_bf16, mid) + _dot(m_bf16, lo)


def _level_sizes(c):
    out, m = [], c // 2
    while m >= 1:
        out.append(m)
        m //= 2
    return out


def _decay_matrix(c):
    blocks = []
    for m in _level_sizes(c):
        mat = np.zeros((c, c), np.float32)
        for t in range(c):
            mid = (t // (2 * m)) * (2 * m) + m
            if t >= mid:
                mat[t, mid:t + 1] = 1.0
            else:
                mat[t, t + 1:mid] = 1.0
        blocks.append(mat)
    blocks.append(np.tril(np.ones((c, c), np.float32)))
    blocks.append(np.triu(np.ones((c, c), np.float32), 1))
    return np.concatenate(blocks, axis=0)


def _eye(n):
    return lax.broadcasted_iota(jnp.int32, (n, n), 0) == lax.broadcasted_iota(jnp.int32, (n, n), 1)


def _pair_masks(c):
    ti = lax.broadcasted_iota(jnp.int32, (c, c), 0)
    si = lax.broadcasted_iota(jnp.int32, (c, c), 1)
    levels = []
    for m in _level_sizes(c):
        same = (ti // (2 * m)) == (si // (2 * m))
        levels.append(same & ((ti & m) != 0) & ((si & m) == 0))
    return levels, ti == si, si <= ti


def _column_of(row, eye):
    return jnp.sum(jnp.where(eye, row, 0.0), axis=1, keepdims=True)


def _row_of(col, eye):
    return jnp.sum(jnp.where(eye, col, 0.0), axis=0, keepdims=True)


def _ffn_body(*refs, final, n_prompt_tiles, split_in, split_out):
    refs = list(refs)
    x_refs = [refs.pop(0) for _ in range(2 if split_in else 1)]
    g_ref, wup_ref, wdn_ref, fin_ref, *o_refs = refs
    is_prompt = pl.program_id(0) < n_prompt_tiles
    if split_in:
        x = jnp.where(is_prompt, x_refs[0][...].reshape(TM_FFN, D_MODEL), x_refs[1][...])
    else:
        x = x_refs[0][...]
    h = _rms(x, g_ref[...]).astype(BF16)
    gu = _dot(h, wup_ref[...])
    act = _silu(gu[:, :D_FF]) * gu[:, D_FF:]
    out = x + 0.5 * _dot(act.astype(BF16), wdn_ref[...])
    if final:
        out = _rms(out, fin_ref[...])
    if split_out:
        @pl.when(is_prompt)
        def _():
            o_refs[0][...] = out.reshape(o_refs[0].shape)

        @pl.when(jnp.logical_not(is_prompt))
        def _():
            o_refs[1][...] = out
    else:
        o_refs[0][...] = out


def _ffn(x, norm, w_up, w_down, final_norm, layer, final, prompt_shape, split_in=False, split_out=False):
    batch, n_chunks = prompt_shape
    n_prompt_tiles = batch * n_chunks * CHUNK // TM_FFN
    seqs_per_tile = TM_FFN // CHUNK
    tiles_per_chunk = batch // seqs_per_tile

    def prompt_idx(i):
        j = jnp.minimum(i, n_prompt_tiles - 1)
        return (j % tiles_per_chunk, j // tiles_per_chunk, 0, 0)

    split_specs = [pl.BlockSpec((seqs_per_tile, None, CHUNK, D_MODEL), prompt_idx),
                   pl.BlockSpec((TM_FFN, D_MODEL), lambda i: (jnp.maximum(i - n_prompt_tiles, 0), 0))]
    joined_spec = pl.BlockSpec((TM_FFN, D_MODEL), lambda i: (i, 0))
    xs = tuple(x) if split_in else (x,)
    n_sample = xs[1].shape[0] if split_in else x.shape[0] - n_prompt_tiles * TM_FFN
    t = n_prompt_tiles * TM_FFN + n_sample
    split_shapes = [jax.ShapeDtypeStruct((batch, n_chunks, CHUNK, D_MODEL), F32),
                    jax.ShapeDtypeStruct((n_sample, D_MODEL), F32)]
    return pl.pallas_call(
        functools.partial(_ffn_body, final=final, n_prompt_tiles=n_prompt_tiles, split_in=split_in,
                          split_out=split_out),
        grid=(t // TM_FFN,),
        in_specs=(split_specs if split_in else [joined_spec]) + [
            pl.BlockSpec((None, 1, D_MODEL), lambda i: (layer, 0, 0)),
            pl.BlockSpec((None, D_MODEL, 2 * D_FF), lambda i: (layer, 0, 0), pipeline_mode=pl.Buffered(1)),
            pl.BlockSpec((None, D_FF, D_MODEL), lambda i: (layer, 0, 0), pipeline_mode=pl.Buffered(1)),
            pl.BlockSpec((1, D_MODEL), lambda i: (0, 0)),
        ],
        out_specs=split_specs if split_out else joined_spec,
        out_shape=split_shapes if split_out else jax.ShapeDtypeStruct((t, D_MODEL), F32),
        compiler_params=_cparams(("arbitrary",)),
        name="ffn",
    )(*xs, norm, w_up, w_down, final_norm)


def _inproj_body(x_ref, g_ref, w_ref, o_ref):
    h = _rms(x_ref[...], g_ref[...]).astype(BF16)
    o_ref[...] = _dot(h, w_ref[...])


def _inproj(x, norm, w_all, layer):
    t = x.shape[0]
    return pl.pallas_call(
        _inproj_body,
        grid=(P_COLS // PROJ_COL_TILE, t // TM_TOK),
        in_specs=[
            pl.BlockSpec((TM_TOK, D_MODEL), lambda j, i: (i, 0)),
            pl.BlockSpec((None, 1, D_MODEL), lambda j, i: (layer, 0, 0)),
            pl.BlockSpec((None, D_MODEL, PROJ_COL_TILE), lambda j, i: (layer, 0, j)),
        ],
        out_specs=pl.BlockSpec((TM_TOK, PROJ_COL_TILE), lambda j, i: (i, j)),
        out_shape=jax.ShapeDtypeStruct((t, P_COLS), F32),
        compiler_params=_cparams(("arbitrary", "arbitrary")),
        name="inproj",
    )(x, norm, w_all)


def _merge_body(x_ref, op_ref, os_ref, g0_ref, g1_ref, g2_ref, wb_ref, wo_ref, o_ref, *, n_prompt_tiles):
    is_prompt = pl.program_id(0) < n_prompt_tiles
    merged = None
    for c, g_ref in enumerate((g0_ref, g1_ref, g2_ref)):
        cs = slice(c * BRANCH_W, (c + 1) * BRANCH_W)
        br = jnp.where(is_prompt, op_ref[:, cs], os_ref[:, cs]).astype(BF16)
        term = jax.nn.sigmoid(g_ref[...]) * _dot(br, wb_ref[c])
        merged = term if merged is None else merged + term
    o_ref[...] = x_ref[...] + _dot(merged.astype(BF16), wo_ref[...])


def _merge(x, o_prompt, o_sample, p_all, w_branch, w_out, layer):
    t = x.shape[0]
    n_prompt_tiles = o_prompt.shape[0] // TM_TOK
    gate_blk = COL_GATE // D_MODEL

    def gate_spec(c):
        return pl.BlockSpec((TM_TOK, D_MODEL), lambda i: (i, gate_blk + c))

    return pl.pallas_call(
        functools.partial(_merge_body, n_prompt_tiles=n_prompt_tiles),
        grid=(t // TM_TOK,),
        in_specs=[
            pl.BlockSpec((TM_TOK, D_MODEL), lambda i: (i, 0)),
            pl.BlockSpec((TM_TOK, N_BRANCH * BRANCH_W), lambda i: (jnp.minimum(i, n_prompt_tiles - 1), 0)),
            pl.BlockSpec((TM_TOK, N_BRANCH * BRANCH_W), lambda i: (0, 0)),
            gate_spec(0), gate_spec(1), gate_spec(2),
            pl.BlockSpec((None, N_BRANCH, BRANCH_W, D_MODEL), lambda i: (layer, 0, 0, 0)),
            pl.BlockSpec((None, D_MODEL, D_MODEL), lambda i: (layer, 0, 0)),
        ],
        out_specs=pl.BlockSpec((TM_TOK, D_MODEL), lambda i: (i, 0)),
        out_shape=jax.ShapeDtypeStruct((t, D_MODEL), F32),
        compiler_params=_cparams(("arbitrary",)),
        name="merge",
    )(x, o_prompt, o_sample, p_all, p_all, p_all, w_branch, w_out)


def _gla_gate_body(sm_ref, wlr_ref, b_ref, o_ref):
    z = _dot(sm_ref[...].astype(BF16), wlr_ref[...]) + b_ref[...]
    o_ref[...] = _log_sigmoid(z) / GLA_GATE_NORM


def _gla_gate(p_all, wlr_pad, gla_b, layer):
    t = p_all.shape[0]
    n = HEADS * GLA_DK
    return pl.pallas_call(
        _gla_gate_body,
        grid=(t // TM_TOK,),
        in_specs=[
            pl.BlockSpec((TM_TOK, SMALL_W), lambda i: (i, COL_SMALL // SMALL_W)),
            pl.BlockSpec((None, SMALL_W, n), lambda i: (layer, 0, 0)),
            pl.BlockSpec((None, 1, n), lambda i: (layer, 0, 0)),
        ],
        out_specs=pl.BlockSpec((TM_TOK, n), lambda i: (i, 0)),
        out_shape=jax.ShapeDtypeStruct((t, n), F32),
        compiler_params=_cparams(("arbitrary",)),
        name="gla_gate",
    )(p_all, wlr_pad, gla_b)


def _layer_lower_bound(lb_raw, layer):
    e = jnp.exp(lb_raw - jnp.max(lb_raw, axis=0, keepdims=True))
    soft = e / jnp.sum(e, axis=0, keepdims=True)
    lb = jnp.zeros_like(soft[0:1])
    for j in range(1, layer + 1):
        lb = lb + soft[j:j + 1]
    return lb


def _head_norm(o, g):
    return o * lax.rsqrt(jnp.mean(o * o, axis=-1, keepdims=True) + EPS) * g


def _gated_linear_branch(q_of, k_of, v_of, logf, dk, dmat, masks, read_state, write_state, emit):
    level_masks, eye, _ = masks
    n_lev = len(level_masks)
    c = eye.shape[0]
    eye_dk = _eye(dk)
    e_all = _exact_dot(dmat, logf)
    for h in range(HEADS):
        ks = slice(h * dk, (h + 1) * dk)
        qh, kh, vh = q_of(h), k_of(h), v_of(h).astype(BF16)
        att = jnp.where(eye, _dot_nt(qh.astype(BF16), kh.astype(BF16)), 0.0)
        for lv in range(n_lev):
            a = jnp.exp(e_all[lv * c:(lv + 1) * c, ks])
            att = att + jnp.where(level_masks[lv], _dot_nt((qh * a).astype(BF16), (kh * a).astype(BF16)), 0.0)
        cum = e_all[n_lev * c:(n_lev + 1) * c, ks]
        rev = e_all[(n_lev + 1) * c:(n_lev + 2) * c, ks]
        st = read_state(h)
        emit(h, _dot(att.astype(BF16), vh) + _dot((qh * jnp.exp(cum)).astype(BF16), st.astype(BF16)))
        decay = _column_of(jnp.exp(cum[c - 1:c, :]), eye_dk)
        write_state(h, st * decay + _dot_tn((kh * jnp.exp(rev)).astype(BF16), vh))


def _mlstm_branch(q_of, k_of, v_of, gates, bcum, masks, read_state, write_state, emit):
    _, eye, tri = masks
    c = eye.shape[0]
    for h in range(HEADS):
        qh, kh, vh = q_of(h), k_of(h), v_of(h).astype(BF16)
        qb = qh.astype(BF16)
        bcol = bcum[:, SM_F + h:SM_F + h + 1]
        icol = gates[:, SM_I + h:SM_I + h + 1]
        log_d = jnp.where(tri, bcol + _row_of(icol - bcol, eye), NEG_BIG)
        cst, nrow, m_prev = read_state(h)
        inter = bcol + m_prev
        m_t = jnp.maximum(inter, jnp.max(log_d, axis=1, keepdims=True))
        d = jnp.exp(log_d - m_t)
        w_inter = jnp.exp(inter - m_t)
        qk = _dot_nt(qb, kh.astype(BF16)) * d
        num = _dot(qk.astype(BF16), vh) + w_inter * _dot(qb, cst.astype(BF16))
        den = jnp.sum(qk, axis=1, keepdims=True) + w_inter * jnp.sum(qh * nrow, axis=1, keepdims=True)
        emit(h, num / jnp.maximum(jnp.abs(den), jnp.exp(-m_t)))
        m_new = m_t[c - 1:c, :]
        b_end = bcol[c - 1:c, :]
        kw = jnp.exp(b_end - bcol + icol - m_new) * kh
        carry = jnp.exp(b_end + m_prev - m_new)
        write_state(h,
                    carry * cst + _dot_tn(kw.astype(BF16), vh),
                    carry * nrow + jnp.sum(kw, axis=0, keepdims=True),
                    m_new)


def _hgrn_log_decay(zf, lb):
    return jnp.log(lb + (1.0 - lb) * jax.nn.sigmoid(zf))


def _mlstm_gates(p_small, ml_bias):
    lane = lax.broadcasted_iota(jnp.int32, p_small.shape, p_small.ndim - 1)
    biased = p_small + ml_bias
    return jnp.where(lane >= SM_F, _log_sigmoid(biased), biased)


def _finish(branch, o, norm_g, gate_pre):
    return _head_norm(o, norm_g) * (jax.nn.sigmoid(gate_pre) if branch == 2 else _silu(gate_pre))


def _gate_pre(load, branch, h):
    name, c0 = (("hg", 3 * HEADS * HG_DK), ("gg", 0), ("ml", 3 * HEADS * ML_DK))[branch]
    return load[name](c0 + h * DV, c0 + (h + 1) * DV)


def _sequence_chunk(load, prm, dmat, c, state_io, emit):
    hg_io, gla_io, ml_io = state_io
    lb = prm["lb"]
    masks = _pair_masks(c)
    n_lev = len(masks[0])
    w, wg = HEADS * HG_DK, HEADS * GLA_DK
    head = lambda h, width=DV: (h * width, (h + 1) * width)
    hg, gqk, ml = load["hg"], load["gqk"], load["ml"]

    _gated_linear_branch(
        lambda h: _silu(hg(*head(h))),
        lambda h: (1.0 - lb[:, slice(*head(h))]) * jax.nn.sigmoid(-hg(w + h * HG_DK, w + (h + 1) * HG_DK)),
        lambda h: hg(2 * w + h * DV, 2 * w + (h + 1) * DV),
        _hgrn_log_decay(hg(w, 2 * w), lb),
        HG_DK, dmat, masks, hg_io[0], hg_io[1], functools.partial(emit, 0))

    _gated_linear_branch(
        lambda h: gqk(*head(h, GLA_DK)) * (GLA_DK ** -0.5),
        lambda h: gqk(wg + h * GLA_DK, wg + (h + 1) * GLA_DK),
        lambda h: load["gv"](*head(h)),
        load["ga"](0, wg),
        GLA_DK, dmat, masks, gla_io[0], gla_io[1], functools.partial(emit, 1))

    gates = _mlstm_gates(load["sm"](0, SMALL_W), prm["ml_bias"])
    _mlstm_branch(
        lambda h: ml(*head(h)),
        lambda h: ml(w + h * ML_DK, w + (h + 1) * ML_DK) * (ML_DK ** -0.5),
        lambda h: ml(2 * w + h * DV, 2 * w + (h + 1) * DV),
        gates, _exact_dot(dmat[n_lev * c:(n_lev + 1) * c, :], gates), masks, ml_io[0], ml_io[1],
        functools.partial(emit, 2))


def _state_io(b, in_refs, out_refs):
    ihg, igla, imc, imn, imm = in_refs
    ohg, ogla, omc, omn, omm = out_refs

    def write_to(ref):
        def write(h, s):
            ref[b, h] = s
        return write

    def ml_write(h, c_new, n_new, m_new):
        omc[b, h] = c_new
        omn[b, h] = n_new
        omm[b, h] = jnp.broadcast_to(m_new, (1, SMALL_W))

    return ((lambda h: ihg[b, h], write_to(ohg)),
            (lambda h: igla[b, h], write_to(ogla)),
            (lambda h: (imc[b, h], imn[b, h], imm[b, h][:, 0:1]), ml_write))


BLOCK_NAMES = ("hg", "ml", "gqk", "gv", "gg", "sm", "ga")
N_BLOCKS = len(BLOCK_NAMES)
PARAM_NAMES = ("hg_norm", "gla_norm", "ml_bias", "ml_norm")
N_PARAMS = 1 + len(PARAM_NAMES)
STATE_TAILS = [(HG_DK, DV), (GLA_DK, DV), (ML_DK, DV), (1, ML_DK), (1, SMALL_W)]
GL_SCRATCH = ("qs", "ks", "qe", "ke", "v", "dec")
ML_SCRATCH = ("qm", "km", "qw", "kw", "v", "emt", "carry")
SAFE_LOG_RANGE = 80.0
ML_SAFE_RANGE = 40.0


class _Tile(NamedTuple):
    n_seq: int
    c: int
    flat: bool
    unroll: int

    def shape(self, width):
        return (self.n_seq * self.c, width) if self.flat else (self.n_seq, self.c, width)

    def load(self, ref, c0, c1):
        if self.flat:
            return ref[:, c0:c1].reshape(self.n_seq, self.c, c1 - c0)
        return ref[:, :, c0:c1]

    def store(self, ref, c0, c1, x):
        if self.flat:
            ref[:, c0:c1] = x.reshape(self.n_seq * self.c, c1 - c0).astype(ref.dtype)
        else:
            ref[:, :, c0:c1] = x.astype(ref.dtype)

    def rows(self, b):
        return pl.ds(pl.multiple_of(b * self.c, self.c), self.c)

    def seq(self, ref, b, c0, c1):
        return ref[self.rows(b), c0:c1] if self.flat else ref[b, :, c0:c1]

    def seq_store(self, ref, b, c0, c1, x):
        if self.flat:
            ref[self.rows(b), c0:c1] = x
        else:
            ref[b, :, c0:c1] = x

    def scan(self, x, combine, fill):
        n_seq, c, w = x.shape
        if self.flat:
            y = x.reshape(n_seq * c, w)
            pos = lax.broadcasted_iota(jnp.int32, y.shape, 0) % c
            shift = 1
            while shift < c:
                y = combine(y, jnp.where(pos >= shift, pltpu.roll(y, shift, 0), fill))
                shift *= 2
            return y.reshape(n_seq, c, w)
        t = lax.broadcasted_iota(jnp.int32, x.shape, 1)
        acc = jnp.full(x.shape, fill, x.dtype)
        for j in range(c):
            acc = combine(acc, jnp.where(t >= j, x[:, j:j + 1, :], fill))
        return acc


def _max_all(x):
    for axis in (2, 1, 0):
        x = jnp.max(x, axis=axis, keepdims=True)
    return x


def _prepare_gated_linear(tile, q, k, v, logf, sc):
    c, n = tile.c, logf.shape[-1]
    cum = tile.scan(logf, jnp.add, 0.0)
    ref, end = cum[:, c // 2 - 1:c // 2, :], cum[:, c - 1:c, :]
    tile.store(sc["qs"], 0, n, q * jnp.exp(cum - ref))
    tile.store(sc["ks"], 0, n, k * jnp.exp(ref - cum))
    tile.store(sc["qe"], 0, n, q * jnp.exp(cum))
    tile.store(sc["ke"], 0, n, k * jnp.exp(end - cum))
    tile.store(sc["v"], 0, v.shape[-1], v)
    decay = jnp.transpose(jnp.exp(cum[:, c - 1, :]))
    for b in range(tile.n_seq):
        sc["dec"][b] = decay[:, b:b + 1]
    return _max_all(jnp.abs(cum - ref))


def _prepare_mlstm(tile, q, k, v, gates, m_state, sc):
    c = tile.c
    lane = lax.broadcasted_iota(jnp.int32, gates.shape, 2)
    g = jnp.where((lane >= SM_I) & (lane < SM_F + HEADS), gates, 0.0)
    bcum = tile.scan(g, jnp.add, 0.0)
    a = pltpu.roll(g, HEADS, 2) - bcum
    lane1 = lax.broadcasted_iota(jnp.int32, (tile.n_seq, 1, SMALL_W), 2)
    m_prev = jnp.zeros((tile.n_seq, 1, SMALL_W), F32)
    for h in range(HEADS):
        m_prev = jnp.where(lane1 == SM_F + h, m_state[:, h], m_prev)
    big_m = jnp.maximum(tile.scan(a, jnp.maximum, NEG_BIG), m_prev)
    ref, m_end = big_m[:, c // 2 - 1:c // 2, :], big_m[:, c - 1:c, :]
    scales = {"qm": jnp.exp(ref - big_m), "km": jnp.exp(a - ref),
              "qw": jnp.exp(m_prev - big_m), "kw": jnp.exp(a - m_end)}
    tile.store(sc["emt"], 0, SMALL_W, jnp.exp(-(bcum + big_m)))
    sc["carry"][...] = jnp.exp(m_prev - m_end)
    for h in range(HEADS):
        h0, h1 = h * ML_DK, (h + 1) * ML_DK
        qh, kh = q[:, :, h0:h1], k[:, :, h0:h1] * (ML_DK ** -0.5)
        for name, x in (("qm", qh), ("km", kh), ("qw", qh), ("kw", kh)):
            tile.store(sc[name], h0, h1, x * scales[name][:, :, SM_F + h:SM_F + h + 1])
    tile.store(sc["v"], 0, v.shape[-1], v)
    spread = jnp.where((lane >= SM_F) & (lane < SM_F + HEADS), jnp.abs(big_m - ref), 0.0)
    return _max_all(spread), bcum[:, c - 1:c, :] + m_end


def _gated_linear_units(tile, b, dk, sc, st_in, st_out, tri, emit):
    for h in range(HEADS):
        k0, k1 = h * dk, (h + 1) * dk
        att = jnp.where(tri, _dot_nt(tile.seq(sc["qs"], b, k0, k1), tile.seq(sc["ks"], b, k0, k1)), 0.0)
        vh = tile.seq(sc["v"], b, h * DV, (h + 1) * DV)
        st = st_in[b, h]
        emit(h, _dot(att.astype(BF16), vh) + _dot(tile.seq(sc["qe"], b, k0, k1), st.astype(BF16)))
        st_out[b, h] = st * sc["dec"][b, k0:k1, :] + _dot_tn(tile.seq(sc["ke"], b, k0, k1), vh)


def _mlstm_units(tile, b, sc, c_in, n_in, c_out, n_out, tri, emit):
    for h in range(HEADS):
        h0, h1 = h * ML_DK, (h + 1) * ML_DK
        att = jnp.where(tri, _dot_nt(tile.seq(sc["qm"], b, h0, h1), tile.seq(sc["km"], b, h0, h1)), 0.0)
        vh, qw, kw = (tile.seq(sc[name], b, h0, h1) for name in ("v", "qw", "kw"))
        cst, nrow = c_in[b, h], n_in[b, h]
        num = _dot(att.astype(BF16), vh) + _dot(qw, cst.astype(BF16))
        den = jnp.sum(att, axis=1, keepdims=True) + jnp.sum(qw.astype(F32) * nrow, axis=1, keepdims=True)
        emit(h, num / jnp.maximum(jnp.abs(den), tile.seq(sc["emt"], b, SM_F + h, SM_F + h + 1)))
        carry = sc["carry"][b][:, SM_F + h:SM_F + h + 1]
        c_out[b, h] = carry * cst + _dot_tn(kw, vh)
        n_out[b, h] = carry * nrow + jnp.sum(kw.astype(F32), axis=0, keepdims=True)


def _scan_body(*refs, layer, tile, zero_init):
    refs = list(refs)
    take_n = lambda n: [refs.pop(0) for _ in range(n)]
    blocks = dict(zip(BLOCK_NAMES, take_n(N_BLOCKS)))
    param_refs = take_n(N_PARAMS)
    dmat_ref, = take_n(1)
    in_states = None if zero_init else take_n(len(STATE_TAILS))
    o_ref, = take_n(1)
    out_states = take_n(len(STATE_TAILS))
    hg_sc = dict(zip(GL_SCRATCH, take_n(len(GL_SCRATCH))))
    gla_sc = dict(zip(GL_SCRATCH, take_n(len(GL_SCRATCH))))
    ml_sc = dict(zip(ML_SCRATCH, take_n(len(ML_SCRATCH))))
    c, n_seq = tile.c, tile.n_seq

    if zero_init:
        in_states = out_states

        @pl.when(pl.program_id(0) == 0)
        def _():
            for r in out_states:
                r[...] = jnp.zeros_like(r)

    lb_ref, *rest = param_refs
    prm = dict(zip(PARAM_NAMES, (r[...] for r in rest)), lb=_layer_lower_bound(lb_ref[...], layer))
    lb = prm["lb"]
    col = {name: functools.partial(tile.load, ref) for name, ref in blocks.items()}
    w, wg = HEADS * HG_DK, HEADS * GLA_DK
    hg, gqk, ml = col["hg"], col["gqk"], col["ml"]

    zf = hg(w, 2 * w)
    gl_spread = jnp.maximum(
        _prepare_gated_linear(tile, _silu(hg(0, w)), (1.0 - lb) * jax.nn.sigmoid(-zf), hg(2 * w, 3 * w),
                              _hgrn_log_decay(zf, lb), hg_sc),
        _prepare_gated_linear(tile, gqk(0, wg) * (GLA_DK ** -0.5), gqk(wg, 2 * wg), col["gv"](0, HEADS * DV),
                              col["ga"](0, wg), gla_sc))
    ml_spread, m_new = _prepare_mlstm(tile, ml(0, w), ml(w, 2 * w), ml(2 * w, 3 * w),
                                      _mlstm_gates(col["sm"](0, SMALL_W), prm["ml_bias"]),
                                      in_states[4][...], ml_sc)
    worst = jnp.maximum(gl_spread * (1.0 / SAFE_LOG_RANGE), ml_spread * (1.0 / ML_SAFE_RANGE))
    tri = _pair_masks(c)[2]

    def emit_for(b):
        def emit(branch, h, val):
            c0 = branch * BRANCH_W + h * DV
            tile.seq_store(o_ref, b, c0, c0 + DV, val)
        return emit

    def factored():
        def per_sequence(b, carry):
            emit = emit_for(b)
            _gated_linear_units(tile, b, HG_DK, hg_sc, in_states[0], out_states[0], tri, functools.partial(emit, 0))
            _gated_linear_units(tile, b, GLA_DK, gla_sc, in_states[1], out_states[1], tri, functools.partial(emit, 1))
            _mlstm_units(tile, b, ml_sc, in_states[2], in_states[3], out_states[2], out_states[3], tri,
                         functools.partial(emit, 2))
            return carry

        lax.fori_loop(0, n_seq, per_sequence, 0, unroll=tile.unroll)
        for h in range(HEADS):
            out_states[4][:, h] = jnp.broadcast_to(m_new[:, :, SM_F + h:SM_F + h + 1], (n_seq, 1, SMALL_W))

    def per_head():
        dmat = dmat_ref[...]

        def per_sequence(b, carry):
            load = {name: functools.partial(tile.seq, ref, b) for name, ref in blocks.items()}
            _sequence_chunk(load, prm, dmat, c, _state_io(b, in_states, out_states), emit_for(b))
            return carry

        lax.fori_loop(0, n_seq, per_sequence, 0)

    lax.cond(worst[0, 0, 0] < 1.0, factored, per_head)

    norms = (prm["hg_norm"], prm["gla_norm"], prm["ml_norm"])
    for branch in range(N_BRANCH):
        for h in range(HEADS):
            c0 = branch * BRANCH_W + h * DV
            tile.store(o_ref, c0, c0 + DV,
                       _finish(branch, tile.load(o_ref, c0, c0 + DV), norms[branch], _gate_pre(col, branch, h)))


def _scan_scratch(tile):
    def gated_linear(dk):
        return ([pltpu.VMEM(tile.shape(HEADS * dk), BF16)] * 4
                + [pltpu.VMEM(tile.shape(HEADS * DV), BF16), pltpu.VMEM((tile.n_seq, HEADS * dk, 1), F32)])

    return (gated_linear(HG_DK) + gated_linear(GLA_DK) + [pltpu.VMEM(tile.shape(HEADS * ML_DK), BF16)] * 5
            + [pltpu.VMEM(tile.shape(SMALL_W), F32), pltpu.VMEM((tile.n_seq, 1, SMALL_W), F32)])


def _param_specs(layer, depth):
    return [
        pl.BlockSpec((depth, HEADS * HG_DK), lambda *_: (0, 0)),
        pl.BlockSpec((None, 1, DV), lambda *_: (layer, 0, 0)),
        pl.BlockSpec((None, 1, DV), lambda *_: (layer, 0, 0)),
        pl.BlockSpec((None, 1, SMALL_W), lambda *_: (layer, 0, 0)),
        pl.BlockSpec((None, 1, DV), lambda *_: (layer, 0, 0)),
    ]


BLOCK_COLS = ((2048, COL_HG), (2048, COL_ML), (512, COL_GLA_QK), (512, COL_GLA_V), (512, COL_GLA_G),
              (SMALL_W, COL_SMALL), (HEADS * GLA_DK, 0))


def _prompt_scan(p_all, log_a, params, dmat, layer, depth, batch, n_chunks):
    tile = _Tile(batch, CHUNK, True, batch)
    rows = batch * CHUNK
    full = lambda shape: pl.BlockSpec(shape, lambda c: (0,) * len(shape))
    state_shapes = [(batch, HEADS) + t for t in STATE_TAILS]
    return pl.pallas_call(
        functools.partial(_scan_body, layer=layer, tile=tile, zero_init=True),
        grid=(n_chunks,),
        in_specs=[pl.BlockSpec((rows, width), lambda c, blk=start // width: (c, blk)) for width, start in BLOCK_COLS]
        + _param_specs(layer, depth) + [full(dmat.shape)],
        out_specs=[pl.BlockSpec((rows, N_BRANCH * BRANCH_W), lambda c: (c, 0))] + [full(s) for s in state_shapes],
        out_shape=[jax.ShapeDtypeStruct((n_chunks * rows, N_BRANCH * BRANCH_W), F32)]
        + [jax.ShapeDtypeStruct(s, F32) for s in state_shapes],
        scratch_shapes=_scan_scratch(tile),
        compiler_params=_cparams(("arbitrary",)),
        name="prompt_scan",
    )(*([p_all] * (N_BLOCKS - 1)), log_a, *params, dmat)


def _sample_scan(p_s, log_a, params, dmat, states, layer, depth):
    n_seq, seq, _ = p_s.shape
    nb = SAMPLE_NB
    tile = _Tile(nb, seq, False, 2)
    full = lambda shape: pl.BlockSpec(shape, lambda i: (0,) * len(shape))
    return pl.pallas_call(
        functools.partial(_scan_body, layer=layer, tile=tile, zero_init=False),
        grid=(n_seq // nb,),
        in_specs=[pl.BlockSpec((nb, seq, width), lambda i, blk=start // width: (i, 0, blk)) for width, start in BLOCK_COLS]
        + _param_specs(layer, depth) + [full(dmat.shape)]
        + [pl.BlockSpec((None, nb, HEADS) + t, lambda i: (layer, i, 0, 0, 0)) for t in STATE_TAILS],
        out_specs=[pl.BlockSpec((nb, seq, N_BRANCH * BRANCH_W), lambda i: (i, 0, 0))]
        + [pl.BlockSpec((nb, HEADS) + t, lambda i: (i, 0, 0, 0)) for t in STATE_TAILS],
        out_shape=[jax.ShapeDtypeStruct((n_seq, seq, N_BRANCH * BRANCH_W), F32)]
        + [jax.ShapeDtypeStruct((n_seq, HEADS) + t, F32) for t in STATE_TAILS],
        scratch_shapes=_scan_scratch(tile),
        compiler_params=_cparams(("arbitrary",)),
        name="sample_scan",
    )(*([p_s] * (N_BLOCKS - 1)), log_a, *params, dmat, *states)


W_IN_MOVES = ((COL_HG, 0, 2048), (COL_ML, 3600, 2048), (COL_GATE, 5656, 3072), (COL_GLA_QK, 2048, 1536),
              (COL_SMALL, 3584, GLA_RANK), (COL_SMALL + GLA_RANK, 5648, 2 * HEADS))
W_IN_USED = COL_SMALL + GLA_RANK + 2 * HEADS
REGROUP_ROWS = 256


def _regroup_body(w_ref, o_ref):
    for dst, src, n in W_IN_MOVES:
        o_ref[:, dst:dst + n] = w_ref[:, src:src + n].astype(BF16)
    o_ref[:, W_IN_USED:] = jnp.zeros((o_ref.shape[0], P_COLS - W_IN_USED), BF16)


def _regroup_w_in(w_in):
    depth, d, d_in = w_in.shape
    assert d_in == sum(n for _, _, n in W_IN_MOVES) and d % REGROUP_ROWS == 0
    return pl.pallas_call(
        _regroup_body,
        grid=(depth, d // REGROUP_ROWS),
        in_specs=[pl.BlockSpec((None, REGROUP_ROWS, d_in), lambda l, i: (l, i, 0))],
        out_specs=pl.BlockSpec((None, REGROUP_ROWS, P_COLS), lambda l, i: (l, i, 0)),
        out_shape=jax.ShapeDtypeStruct((depth, d, P_COLS), BF16),
        compiler_params=_cparams(("arbitrary", "arbitrary")),
        name="regroup_w_in",
    )(w_in)


def kernel(x_prompt, x_sample, state_hgrn, state_gla, state_mlstm_C, state_mlstm_n, state_mlstm_m,
           ffn1_norm, ffn1_w_up, ffn1_w_down, mix_norm, w_in, hgrn_lb_raw, hgrn_out_norm,
           gla_w_gate_lr, gla_b_gate, gla_out_norm, mlstm_b_i, mlstm_b_f, mlstm_out_norm,
           w_branch, w_out, ffn2_norm, ffn2_w_up, ffn2_w_down, final_norm):
    depth = w_in.shape[0]
    batch, seq, _ = x_prompt.shape
    n_seq, dec_seq, _ = x_sample.shape
    assert seq % CHUNK == 0 and dec_seq % CHUNK != 0 and dec_seq & (dec_seq - 1) == 0
    assert (batch * CHUNK) % TM_TOK == 0 and n_seq * dec_seq == TM_TOK and n_seq % SAMPLE_NB == 0
    n_chunks = seq // CHUNK
    n_prompt = batch * seq

    assert TM_FFN % CHUNK == 0 and batch % (TM_FFN // CHUNK) == 0 and (n_seq * dec_seq) % TM_FFN == 0
    x = (x_prompt.reshape(batch, n_chunks, CHUNK, D_MODEL), x_sample.reshape(n_seq * dec_seq, D_MODEL))

    row3 = lambda a: a.reshape(a.shape[0], 1, a.shape[-1])
    w_all = _regroup_w_in(w_in)
    wlr_pad = jnp.pad(gla_w_gate_lr, ((0, 0), (0, SMALL_W - GLA_RANK), (0, 0))).astype(BF16)
    ml_bias = jnp.pad(jnp.concatenate([mlstm_b_i, mlstm_b_f], axis=-1),
                      ((0, 0), (SM_I, SMALL_W - SM_I - 2 * HEADS)))
    scan_params = (hgrn_lb_raw, row3(hgrn_out_norm), row3(gla_out_norm), row3(ml_bias), row3(mlstm_out_norm))
    gla_b3 = row3(gla_b_gate)
    dmat_p = jnp.asarray(_decay_matrix(CHUNK), BF16)
    dmat_s = jnp.asarray(_decay_matrix(dec_seq), BF16)
    sample_states = (state_hgrn, state_gla, state_mlstm_C,
                     state_mlstm_n.reshape(depth, n_seq, HEADS, 1, ML_DK),
                     jnp.broadcast_to(state_mlstm_m[..., None, None], (depth, n_seq, HEADS, 1, SMALL_W)))
    ffn_w = [(row3(ffn1_norm), ffn1_w_up.astype(BF16), ffn1_w_down.astype(BF16)),
             (row3(ffn2_norm), ffn2_w_up.astype(BF16), ffn2_w_down.astype(BF16))]
    w_branch_b, w_out_b, mix_norm3 = w_branch.astype(BF16), w_out.astype(BF16), row3(mix_norm)
    fin = final_norm.reshape(1, D_MODEL)

    p_states, s_states = [], []
    for l in range(depth):
        x = _ffn(x, *ffn_w[0], fin, l, False, (batch, n_chunks), split_in=l == 0)
        p_all = _inproj(x, mix_norm3, w_all, l)
        log_a = _gla_gate(p_all, wlr_pad, gla_b3, l)
        o_p, *ps = _prompt_scan(p_all, log_a, scan_params, dmat_p, l, depth, batch, n_chunks)
        p_s = p_all[n_prompt:].reshape(n_seq, dec_seq, P_COLS)
        log_a_s = log_a[n_prompt:].reshape(n_seq, dec_seq, HEADS * GLA_DK)
        o_s, *ss = _sample_scan(p_s, log_a_s, scan_params, dmat_s, sample_states, l, depth)
        x = _merge(x, o_p, o_s.reshape(n_seq * dec_seq, N_BRANCH * BRANCH_W), p_all, w_branch_b, w_out_b, l)
        last = l == depth - 1
        x = _ffn(x, *ffn_w[1], fin, l, last, (batch, n_chunks), split_out=last)
        p_states.append(ps)
        s_states.append(ss)

    y_prompt = x[0].reshape(batch, seq, D_MODEL)
    y_sample = x[1].reshape(n_seq, dec_seq, D_MODEL)

    def stacked(states):
        hg, gla, mc, mn, mm = (jnp.stack([st[i] for st in states]) for i in range(len(STATE_TAILS)))
        return hg, gla, mc, mn[..., 0, :], mm[..., 0, 0]

    return (y_prompt, y_sample) + stacked(p_states) + stacked(s_states)
```

```python
import functools
from typing import NamedTuple

import numpy as np
import jax
import jax.numpy as jnp
from jax import lax
from jax.experimental import pallas as pl
from jax.experimental.pallas import tpu as pltpu

D_MODEL = 1024
HEADS = 4
HG_DK = 128
GLA_DK = 64
GLA_RANK = 16
GLA_GATE_NORM = 16.0
ML_DK = 128
DV = 128
BRANCH_W = 512
N_BRANCH = 3
D_FF = 2816
CHUNK = 64
EPS = 1e-6
NEG_BIG = -1e30

F32 = jnp.float32
BF16 = jnp.bfloat16

COL_HG = 0
COL_ML = 2048
COL_GATE = 4096
COL_GLA_QK = 7168
COL_GLA_V = 7680
COL_GLA_G = 8192
COL_SMALL = 8704
P_COLS = 8832
SMALL_W = 128
SM_I = GLA_RANK
SM_F = GLA_RANK + HEADS

TM_FFN = 512
TM_TOK = 512
PROJ_COL_TILE = 2944
GATE_MAX_TILES = 11
SAMPLE_NB = 8
VMEM_LIMIT = 56 * 1024 * 1024


def _cparams(sem):
    return pltpu.CompilerParams(dimension_semantics=sem, vmem_limit_bytes=VMEM_LIMIT)


def _dot(a, b):
    return jnp.dot(a, b, preferred_element_type=F32)


def _dot_nt(a, b):
    return lax.dot_general(a, b, (((1,), (1,)), ((), ())), preferred_element_type=F32)


def _dot_tn(a, b):
    return lax.dot_general(a, b, (((0,), (0,)), ((), ())), preferred_element_type=F32)


def _rms(x, g):
    return x * lax.rsqrt(jnp.mean(x * x, axis=-1, keepdims=True) + EPS) * g


def _log_sigmoid(x):
    return jnp.minimum(x, 0.0) - jnp.log1p(jnp.exp(-jnp.abs(x)))


def _silu(x):
    return x * jax.nn.sigmoid(x)


def _exact_dot(m_bf16, x):
    hi = x.astype(BF16)
    r1 = x - hi.astype(F32)
    mid = r1.astype(BF16)
    lo = (r1 - mid.astype(F32)).astype(BF16)
    return _dot(m_bf16, hi) + _dot(m_bf16, mid) + _dot(m_bf16, lo)


def _level_sizes(c):
    out, m = [], c // 2
    while m >= 1:
        out.append(m)
        m //= 2
    return out


def _decay_matrix(c):
    blocks = []
    for m in _level_sizes(c):
        mat = np.zeros((c, c), np.float32)
        for t in range(c):
            mid = (t // (2 * m)) * (2 * m) + m
            if t >= mid:
                mat[t, mid:t + 1] = 1.0
            else:
                mat[t, t + 1:mid] = 1.0
        blocks.append(mat)
    blocks.append(np.tril(np.ones((c, c), np.float32)))
    blocks.append(np.triu(np.ones((c, c), np.float32), 1))
    return np.concatenate(blocks, axis=0)


def _eye(n):
    return lax.broadcasted_iota(jnp.int32, (n, n), 0) == lax.broadcasted_iota(jnp.int32, (n, n), 1)


def _pair_masks(c):
    ti = lax.broadcasted_iota(jnp.int32, (c, c), 0)
    si = lax.broadcasted_iota(jnp.int32, (c, c), 1)
    levels = []
    for m in _level_sizes(c):
        same = (ti // (2 * m)) == (si // (2 * m))
        levels.append(same & ((ti & m) != 0) & ((si & m) == 0))
    return levels, ti == si, si <= ti


def _column_of(row, eye):
    return jnp.sum(jnp.where(eye, row, 0.0), axis=1, keepdims=True)


def _row_of(col, eye):
    return jnp.sum(jnp.where(eye, col, 0.0), axis=0, keepdims=True)


def _ffn_body(*refs, final, n_prompt_tiles, split_in, split_out):
    refs = list(refs)
    x_refs = [refs.pop(0) for _ in range(2 if split_in else 1)]
    g_ref, wup_ref, wdn_ref, fin_ref, *o_refs = refs
    is_prompt = pl.program_id(0) < n_prompt_tiles
    if split_in:
        x = jnp.where(is_prompt, x_refs[0][...].reshape(TM_FFN, D_MODEL), x_refs[1][...])
    else:
        x = x_refs[0][...]
    h = _rms(x, g_ref[...]).astype(BF16)
    gu = _dot(h, wup_ref[...])
    act = _silu(gu[:, :D_FF]) * gu[:, D_FF:]
    out = x + 0.5 * _dot(act.astype(BF16), wdn_ref[...])
    if final:
        out = _rms(out, fin_ref[...])
    if split_out:
        @pl.when(is_prompt)
        def _():
            o_refs[0][...] = out.reshape(o_refs[0].shape)

        @pl.when(jnp.logical_not(is_prompt))
        def _():
            o_refs[1][...] = out
    else:
        o_refs[0][...] = out


def _ffn(x, norm, w_up, w_down, final_norm, layer, final, prompt_shape, split_in=False, split_out=False):
    batch, n_chunks = prompt_shape
    n_prompt_tiles = batch * n_chunks * CHUNK // TM_FFN
    seqs_per_tile = TM_FFN // CHUNK
    tiles_per_chunk = batch // seqs_per_tile

    def prompt_idx(i):
        j = jnp.minimum(i, n_prompt_tiles - 1)
        return (j % tiles_per_chunk, j // tiles_per_chunk, 0, 0)

    split_specs = [pl.BlockSpec((seqs_per_tile, None, CHUNK, D_MODEL), prompt_idx),
                   pl.BlockSpec((TM_FFN, D_MODEL), lambda i: (jnp.maximum(i - n_prompt_tiles, 0), 0))]
    joined_spec = pl.BlockSpec((TM_FFN, D_MODEL), lambda i: (i, 0))
    xs = tuple(x) if split_in else (x,)
    n_sample = xs[1].shape[0] if split_in else x.shape[0] - n_prompt_tiles * TM_FFN
    t = n_prompt_tiles * TM_FFN + n_sample
    split_shapes = [jax.ShapeDtypeStruct((batch, n_chunks, CHUNK, D_MODEL), F32),
                    jax.ShapeDtypeStruct((n_sample, D_MODEL), F32)]
    return pl.pallas_call(
        functools.partial(_ffn_body, final=final, n_prompt_tiles=n_prompt_tiles, split_in=split_in,
                          split_out=split_out),
        grid=(t // TM_FFN,),
        in_specs=(split_specs if split_in else [joined_spec]) + [
            pl.BlockSpec((None, 1, D_MODEL), lambda i: (layer, 0, 0)),
            pl.BlockSpec((None, D_MODEL, 2 * D_FF), lambda i: (layer, 0, 0), pipeline_mode=pl.Buffered(1)),
            pl.BlockSpec((None, D_FF, D_MODEL), lambda i: (layer, 0, 0), pipeline_mode=pl.Buffered(1)),
            pl.BlockSpec((1, D_MODEL), lambda i: (0, 0)),
        ],
        out_specs=split_specs if split_out else joined_spec,
        out_shape=split_shapes if split_out else jax.ShapeDtypeStruct((t, D_MODEL), F32),
        compiler_params=_cparams(("arbitrary",)),
        name="ffn",
    )(*xs, norm, w_up, w_down, final_norm)


def _inproj_body(x_ref, g_ref, w_ref, o_ref):
    h = _rms(x_ref[...], g_ref[...]).astype(BF16)
    o_ref[...] = _dot(h, w_ref[...])


def _inproj(x, norm, w_all, layer):
    t = x.shape[0]
    return pl.pallas_call(
        _inproj_body,
        grid=(P_COLS // PROJ_COL_TILE, t // TM_TOK),
        in_specs=[
            pl.BlockSpec((TM_TOK, D_MODEL), lambda j, i: (i, 0)),
            pl.BlockSpec((None, 1, D_MODEL), lambda j, i: (layer, 0, 0)),
            pl.BlockSpec((None, D_MODEL, PROJ_COL_TILE), lambda j, i: (layer, 0, j)),
        ],
        out_specs=pl.BlockSpec((TM_TOK, PROJ_COL_TILE), lambda j, i: (i, j)),
        out_shape=jax.ShapeDtypeStruct((t, P_COLS), F32),
        compiler_params=_cparams(("arbitrary", "arbitrary")),
        name="inproj",
    )(x, norm, w_all)


def _merge_body(x_ref, op_ref, os_ref, g0_ref, g1_ref, g2_ref, wb_ref, wo_ref, o_ref, *, n_prompt_tiles):
    is_prompt = pl.program_id(0) < n_prompt_tiles
    merged = None
    for c, g_ref in enumerate((g0_ref, g1_ref, g2_ref)):
        cs = slice(c * BRANCH_W, (c + 1) * BRANCH_W)
        br = jnp.where(is_prompt, op_ref[:, cs], os_ref[:, cs]).astype(BF16)
        term = jax.nn.sigmoid(g_ref[...]) * _dot(br, wb_ref[c])
        merged = term if merged is None else merged + term
    o_ref[...] = x_ref[...] + _dot(merged.astype(BF16), wo_ref[...])


def _merge(x, o_prompt, o_sample, p_all, w_branch, w_out, layer):
    t = x.shape[0]
    n_prompt_tiles = o_prompt.shape[0] // TM_TOK
    gate_blk = COL_GATE // D_MODEL

    def gate_spec(c):
        return pl.BlockSpec((TM_TOK, D_MODEL), lambda i: (i, gate_blk + c))

    return pl.pallas_call(
        functools.partial(_merge_body, n_prompt_tiles=n_prompt_tiles),
        grid=(t // TM_TOK,),
        in_specs=[
            pl.BlockSpec((TM_TOK, D_MODEL), lambda i: (i, 0)),
            pl.BlockSpec((TM_TOK, N_BRANCH * BRANCH_W), lambda i: (jnp.minimum(i, n_prompt_tiles - 1), 0)),
            pl.BlockSpec((TM_TOK, N_BRANCH * BRANCH_W), lambda i: (0, 0)),
            gate_spec(0), gate_spec(1), gate_spec(2),
            pl.BlockSpec((None, N_BRANCH, BRANCH_W, D_MODEL), lambda i: (layer, 0, 0, 0)),
            pl.BlockSpec((None, D_MODEL, D_MODEL), lambda i: (layer, 0, 0)),
        ],
        out_specs=pl.BlockSpec((TM_TOK, D_MODEL), lambda i: (i, 0)),
        out_shape=jax.ShapeDtypeStruct((t, D_MODEL), F32),
        compiler_params=_cparams(("arbitrary",)),
        name="merge",
    )(x, o_prompt, o_sample, p_all, p_all, p_all, w_branch, w_out)


def _gla_gate_body(sm_ref, wlr_ref, b_ref, o_ref):
    z = _dot(sm_ref[...].astype(BF16), wlr_ref[...]) + b_ref[...]
    o_ref[...] = _log_sigmoid(z) / GLA_GATE_NORM


def _gla_gate(p_all, wlr_pad, gla_b, layer):
    t = p_all.shape[0]
    n = HEADS * GLA_DK
    tm = TM_TOK * max(d for d in range(1, GATE_MAX_TILES + 1) if (t // TM_TOK) % d == 0)
    return pl.pallas_call(
        _gla_gate_body,
        grid=(t // tm,),
        in_specs=[
            pl.BlockSpec((tm, SMALL_W), lambda i: (i, COL_SMALL // SMALL_W)),
            pl.BlockSpec((None, SMALL_W, n), lambda i: (layer, 0, 0)),
            pl.BlockSpec((None, 1, n), lambda i: (layer, 0, 0)),
        ],
        out_specs=pl.BlockSpec((tm, n), lambda i: (i, 0)),
        out_shape=jax.ShapeDtypeStruct((t, n), F32),
        compiler_params=_cparams(("arbitrary",)),
        name="gla_gate",
    )(p_all, wlr_pad, gla_b)


def _layer_lower_bound(lb_raw, layer):
    e = jnp.exp(lb_raw - jnp.max(lb_raw, axis=0, keepdims=True))
    soft = e / jnp.sum(e, axis=0, keepdims=True)
    lb = jnp.zeros_like(soft[0:1])
    for j in range(1, layer + 1):
        lb = lb + soft[j:j + 1]
    return lb


def _head_norm(o, g):
    return o * lax.rsqrt(jnp.mean(o * o, axis=-1, keepdims=True) + EPS) * g


def _gated_linear_branch(q_of, k_of, v_of, logf, dk, dmat, masks, read_state, write_state, emit):
    level_masks, eye, _ = masks
    n_lev = len(level_masks)
    c = eye.shape[0]
    eye_dk = _eye(dk)
    e_all = _exact_dot(dmat, logf)
    for h in range(HEADS):
        ks = slice(h * dk, (h + 1) * dk)
        qh, kh, vh = q_of(h), k_of(h), v_of(h).astype(BF16)
        att = jnp.where(eye, _dot_nt(qh.astype(BF16), kh.astype(BF16)), 0.0)
        for lv in range(n_lev):
            a = jnp.exp(e_all[lv * c:(lv + 1) * c, ks])
            att = att + jnp.where(level_masks[lv], _dot_nt((qh * a).astype(BF16), (kh * a).astype(BF16)), 0.0)
        cum = e_all[n_lev * c:(n_lev + 1) * c, ks]
        rev = e_all[(n_lev + 1) * c:(n_lev + 2) * c, ks]
        st = read_state(h)
        emit(h, _dot(att.astype(BF16), vh) + _dot((qh * jnp.exp(cum)).astype(BF16), st.astype(BF16)))
        decay = _column_of(jnp.exp(cum[c - 1:c, :]), eye_dk)
        write_state(h, st * decay + _dot_tn((kh * jnp.exp(rev)).astype(BF16), vh))


def _mlstm_branch(q_of, k_of, v_of, gates, bcum, masks, read_state, write_state, emit):
    _, eye, tri = masks
    c = eye.shape[0]
    for h in range(HEADS):
        qh, kh, vh = q_of(h), k_of(h), v_of(h).astype(BF16)
        qb = qh.astype(BF16)
        bcol = bcum[:, SM_F + h:SM_F + h + 1]
        icol = gates[:, SM_I + h:SM_I + h + 1]
        log_d = jnp.where(tri, bcol + _row_of(icol - bcol, eye), NEG_BIG)
        cst, nrow, m_prev = read_state(h)
        inter = bcol + m_prev
        m_t = jnp.maximum(inter, jnp.max(log_d, axis=1, keepdims=True))
        d = jnp.exp(log_d - m_t)
        w_inter = jnp.exp(inter - m_t)
        qk = _dot_nt(qb, kh.astype(BF16)) * d
        num = _dot(qk.astype(BF16), vh) + w_inter * _dot(qb, cst.astype(BF16))
        den = jnp.sum(qk, axis=1, keepdims=True) + w_inter * jnp.sum(qh * nrow, axis=1, keepdims=True)
        emit(h, num / jnp.maximum(jnp.abs(den), jnp.exp(-m_t)))
        m_new = m_t[c - 1:c, :]
        b_end = bcol[c - 1:c, :]
        kw = jnp.exp(b_end - bcol + icol - m_new) * kh
        carry = jnp.exp(b_end + m_prev - m_new)
        write_state(h,
                    carry * cst + _dot_tn(kw.astype(BF16), vh),
                    carry * nrow + jnp.sum(kw, axis=0, keepdims=True),
                    m_new)


def _hgrn_log_decay(zf, lb):
    return jnp.log(lb + (1.0 - lb) * jax.nn.sigmoid(zf))


def _mlstm_gates(p_small, ml_bias):
    lane = lax.broadcasted_iota(jnp.int32, p_small.shape, p_small.ndim - 1)
    biased = p_small + ml_bias
    return jnp.where(lane >= SM_F, _log_sigmoid(biased), biased)


def _finish(branch, o, norm_g, gate_pre):
    return _head_norm(o, norm_g) * (jax.nn.sigmoid(gate_pre) if branch == 2 else _silu(gate_pre))


def _gate_pre(load, branch, h):
    name, c0 = (("hg", 3 * HEADS * HG_DK), ("gg", 0), ("ml", 3 * HEADS * ML_DK))[branch]
    return load[name](c0 + h * DV, c0 + (h + 1) * DV)


def _sequence_chunk(load, prm, dmat, c, state_io, emit):
    hg_io, gla_io, ml_io = state_io
    lb = prm["lb"]
    masks = _pair_masks(c)
    n_lev = len(masks[0])
    w, wg = HEADS * HG_DK, HEADS * GLA_DK
    head = lambda h, width=DV: (h * width, (h + 1) * width)
    hg, gqk, ml = load["hg"], load["gqk"], load["ml"]

    _gated_linear_branch(
        lambda h: _silu(hg(*head(h))),
        lambda h: (1.0 - lb[:, slice(*head(h))]) * jax.nn.sigmoid(-hg(w + h * HG_DK, w + (h + 1) * HG_DK)),
        lambda h: hg(2 * w + h * DV, 2 * w + (h + 1) * DV),
        _hgrn_log_decay(hg(w, 2 * w), lb),
        HG_DK, dmat, masks, hg_io[0], hg_io[1], functools.partial(emit, 0))

    _gated_linear_branch(
        lambda h: gqk(*head(h, GLA_DK)) * (GLA_DK ** -0.5),
        lambda h: gqk(wg + h * GLA_DK, wg + (h + 1) * GLA_DK),
        lambda h: load["gv"](*head(h)),
        load["ga"](0, wg),
        GLA_DK, dmat, masks, gla_io[0], gla_io[1], functools.partial(emit, 1))

    gates = _mlstm_gates(load["sm"](0, SMALL_W), prm["ml_bias"])
    _mlstm_branch(
        lambda h: ml(*head(h)),
        lambda h: ml(w + h * ML_DK, w + (h + 1) * ML_DK) * (ML_DK ** -0.5),
        lambda h: ml(2 * w + h * DV, 2 * w + (h + 1) * DV),
        gates, _exact_dot(dmat[n_lev * c:(n_lev + 1) * c, :], gates), masks, ml_io[0], ml_io[1],
        functools.partial(emit, 2))


def _state_io(b, in_refs, out_refs):
    ihg, igla, imc, imn, imm = in_refs
    ohg, ogla, omc, omn, omm = out_refs

    def write_to(ref):
        def write(h, s):
            ref[b, h] = s
        return write

    def ml_write(h, c_new, n_new, m_new):
        omc[b, h] = c_new
        omn[b, h] = n_new
        omm[b, h] = jnp.broadcast_to(m_new, (1, SMALL_W))

    return ((lambda h: ihg[b, h], write_to(ohg)),
            (lambda h: igla[b, h], write_to(ogla)),
            (lambda h: (imc[b, h], imn[b, h], imm[b, h][:, 0:1]), ml_write))


BLOCK_NAMES = ("hg", "ml", "gqk", "gv", "gg", "sm", "ga")
N_BLOCKS = len(BLOCK_NAMES)
PARAM_NAMES = ("hg_norm", "gla_norm", "ml_bias", "ml_norm")
N_PARAMS = 1 + len(PARAM_NAMES)
STATE_TAILS = [(HG_DK, DV), (GLA_DK, DV), (ML_DK, DV), (1, ML_DK), (1, SMALL_W)]
GL_SCRATCH = ("qs", "ks", "qe", "ke", "v", "dec")
ML_SCRATCH = ("qm", "km", "qw", "kw", "v", "emt", "carry")
SAFE_LOG_RANGE = 80.0
ML_SAFE_RANGE = 40.0


class _Tile(NamedTuple):
    n_seq: int
    c: int
    flat: bool
    unroll: int

    def shape(self, width):
        return (self.n_seq * self.c, width) if self.flat else (self.n_seq, self.c, width)

    def load(self, ref, c0, c1):
        if self.flat:
            return ref[:, c0:c1].reshape(self.n_seq, self.c, c1 - c0)
        return ref[:, :, c0:c1]

    def store(self, ref, c0, c1, x):
        if self.flat:
            ref[:, c0:c1] = x.reshape(self.n_seq * self.c, c1 - c0).astype(ref.dtype)
        else:
            ref[:, :, c0:c1] = x.astype(ref.dtype)

    def rows(self, b):
        return pl.ds(pl.multiple_of(b * self.c, self.c), self.c)

    def seq(self, ref, b, c0, c1):
        return ref[self.rows(b), c0:c1] if self.flat else ref[b, :, c0:c1]

    def seq_store(self, ref, b, c0, c1, x):
        if self.flat:
            ref[self.rows(b), c0:c1] = x
        else:
            ref[b, :, c0:c1] = x

    def scan(self, x, combine, fill):
        n_seq, c, w = x.shape
        if self.flat:
            y = x.reshape(n_seq * c, w)
            pos = lax.broadcasted_iota(jnp.int32, y.shape, 0) % c
            shift = 1
            while shift < c:
                y = combine(y, jnp.where(pos >= shift, pltpu.roll(y, shift, 0), fill))
                shift *= 2
            return y.reshape(n_seq, c, w)
        t = lax.broadcasted_iota(jnp.int32, x.shape, 1)
        acc = jnp.full(x.shape, fill, x.dtype)
        for j in range(c):
            acc = combine(acc, jnp.where(t >= j, x[:, j:j + 1, :], fill))
        return acc


def _max_all(x):
    for axis in (2, 1, 0):
        x = jnp.max(x, axis=axis, keepdims=True)
    return x


def _prepare_gated_linear(tile, q, k, v, logf, sc):
    c, n = tile.c, logf.shape[-1]
    cum = tile.scan(logf, jnp.add, 0.0)
    ref, end = cum[:, c // 2 - 1:c // 2, :], cum[:, c - 1:c, :]
    tile.store(sc["qs"], 0, n, q * jnp.exp(cum - ref))
    tile.store(sc["ks"], 0, n, k * jnp.exp(ref - cum))
    tile.store(sc["qe"], 0, n, q * jnp.exp(cum))
    tile.store(sc["ke"], 0, n, k * jnp.exp(end - cum))
    tile.store(sc["v"], 0, v.shape[-1], v)
    decay = jnp.transpose(jnp.exp(cum[:, c - 1, :]))
    for b in range(tile.n_seq):
        sc["dec"][b] = decay[:, b:b + 1]
    return _max_all(jnp.abs(cum - ref))


def _prepare_mlstm(tile, q, k, v, gates, m_state, sc):
    c = tile.c
    lane = lax.broadcasted_iota(jnp.int32, gates.shape, 2)
    g = jnp.where((lane >= SM_I) & (lane < SM_F + HEADS), gates, 0.0)
    bcum = tile.scan(g, jnp.add, 0.0)
    a = pltpu.roll(g, HEADS, 2) - bcum
    lane1 = lax.broadcasted_iota(jnp.int32, (tile.n_seq, 1, SMALL_W), 2)
    m_prev = jnp.zeros((tile.n_seq, 1, SMALL_W), F32)
    for h in range(HEADS):
        m_prev = jnp.where(lane1 == SM_F + h, m_state[:, h], m_prev)
    big_m = jnp.maximum(tile.scan(a, jnp.maximum, NEG_BIG), m_prev)
    ref, m_end = big_m[:, c // 2 - 1:c // 2, :], big_m[:, c - 1:c, :]
    scales = {"qm": jnp.exp(ref - big_m), "km": jnp.exp(a - ref),
              "qw": jnp.exp(m_prev - big_m), "kw": jnp.exp(a - m_end)}
    tile.store(sc["emt"], 0, SMALL_W, jnp.exp(-(bcum + big_m)))
    sc["carry"][...] = jnp.exp(m_prev - m_end)
    for h in range(HEADS):
        h0, h1 = h * ML_DK, (h + 1) * ML_DK
        qh, kh = q[:, :, h0:h1], k[:, :, h0:h1] * (ML_DK ** -0.5)
        for name, x in (("qm", qh), ("km", kh), ("qw", qh), ("kw", kh)):
            tile.store(sc[name], h0, h1, x * scales[name][:, :, SM_F + h:SM_F + h + 1])
    tile.store(sc["v"], 0, v.shape[-1], v)
    spread = jnp.where((lane >= SM_F) & (lane < SM_F + HEADS), jnp.abs(big_m - ref), 0.0)
    return _max_all(spread), bcum[:, c - 1:c, :] + m_end


def _gated_linear_units(tile, b, dk, sc, st_in, st_out, tri, emit):
    for h in range(HEADS):
        k0, k1 = h * dk, (h + 1) * dk
        att = jnp.where(tri, _dot_nt(tile.seq(sc["qs"], b, k0, k1), tile.seq(sc["ks"], b, k0, k1)), 0.0)
        vh = tile.seq(sc["v"], b, h * DV, (h + 1) * DV)
        st = st_in[b, h]
        emit(h, _dot(att.astype(BF16), vh) + _dot(tile.seq(sc["qe"], b, k0, k1), st.astype(BF16)))
        st_out[b, h] = st * sc["dec"][b, k0:k1, :] + _dot_tn(tile.seq(sc["ke"], b, k0, k1), vh)


def _mlstm_units(tile, b, sc, c_in, n_in, c_out, n_out, tri, emit):
    for h in range(HEADS):
        h0, h1 = h * ML_DK, (h + 1) * ML_DK
        att = jnp.where(tri, _dot_nt(tile.seq(sc["qm"], b, h0, h1), tile.seq(sc["km"], b, h0, h1)), 0.0)
        vh, qw, kw = (tile.seq(sc[name], b, h0, h1) for name in ("v", "qw", "kw"))
        cst, nrow = c_in[b, h], n_in[b, h]
        num = _dot(att.astype(BF16), vh) + _dot(qw, cst.astype(BF16))
        den = jnp.sum(att, axis=1, keepdims=True) + jnp.sum(qw.astype(F32) * nrow, axis=1, keepdims=True)
        emit(h, num / jnp.maximum(jnp.abs(den), tile.seq(sc["emt"], b, SM_F + h, SM_F + h + 1)))
        carry = sc["carry"][b][:, SM_F + h:SM_F + h + 1]
        c_out[b, h] = carry * cst + _dot_tn(kw, vh)
        n_out[b, h] = carry * nrow + jnp.sum(kw.astype(F32), axis=0, keepdims=True)


def _scan_body(*refs, layer, tile, zero_init, n_stacked=0):
    refs = list(refs)
    take_n = lambda n: [refs.pop(0) for _ in range(n)]
    blocks = dict(zip(BLOCK_NAMES, take_n(N_BLOCKS)))
    param_refs = take_n(N_PARAMS)
    dmat_ref, = take_n(1)
    in_states = None if zero_init else take_n(len(STATE_TAILS))
    take_n(n_stacked)
    o_ref, = take_n(1)
    out_states = take_n(len(STATE_TAILS))
    hg_sc = dict(zip(GL_SCRATCH, take_n(len(GL_SCRATCH))))
    gla_sc = dict(zip(GL_SCRATCH, take_n(len(GL_SCRATCH))))
    ml_sc = dict(zip(ML_SCRATCH, take_n(len(ML_SCRATCH))))
    c, n_seq = tile.c, tile.n_seq

    if zero_init:
        in_states = out_states

        @pl.when(pl.program_id(0) == 0)
        def _():
            for r in out_states:
                r[...] = jnp.zeros_like(r)

    lb_ref, *rest = param_refs
    prm = dict(zip(PARAM_NAMES, (r[...] for r in rest)), lb=_layer_lower_bound(lb_ref[...], layer))
    lb = prm["lb"]
    col = {name: functools.partial(tile.load, ref) for name, ref in blocks.items()}
    w, wg = HEADS * HG_DK, HEADS * GLA_DK
    hg, gqk, ml = col["hg"], col["gqk"], col["ml"]

    zf = hg(w, 2 * w)
    gl_spread = jnp.maximum(
        _prepare_gated_linear(tile, _silu(hg(0, w)), (1.0 - lb) * jax.nn.sigmoid(-zf), hg(2 * w, 3 * w),
                              _hgrn_log_decay(zf, lb), hg_sc),
        _prepare_gated_linear(tile, gqk(0, wg) * (GLA_DK ** -0.5), gqk(wg, 2 * wg), col["gv"](0, HEADS * DV),
                              col["ga"](0, wg), gla_sc))
    ml_spread, m_new = _prepare_mlstm(tile, ml(0, w), ml(w, 2 * w), ml(2 * w, 3 * w),
                                      _mlstm_gates(col["sm"](0, SMALL_W), prm["ml_bias"]),
                                      in_states[4][...], ml_sc)
    worst = jnp.maximum(gl_spread * (1.0 / SAFE_LOG_RANGE), ml_spread * (1.0 / ML_SAFE_RANGE))
    tri = _pair_masks(c)[2]

    def emit_for(b):
        def emit(branch, h, val):
            c0 = branch * BRANCH_W + h * DV
            tile.seq_store(o_ref, b, c0, c0 + DV, val)
        return emit

    def factored():
        def per_sequence(b, carry):
            emit = emit_for(b)
            _gated_linear_units(tile, b, HG_DK, hg_sc, in_states[0], out_states[0], tri, functools.partial(emit, 0))
            _gated_linear_units(tile, b, GLA_DK, gla_sc, in_states[1], out_states[1], tri, functools.partial(emit, 1))
            _mlstm_units(tile, b, ml_sc, in_states[2], in_states[3], out_states[2], out_states[3], tri,
                         functools.partial(emit, 2))
            return carry

        lax.fori_loop(0, n_seq, per_sequence, 0, unroll=tile.unroll)
        for h in range(HEADS):
            out_states[4][:, h] = jnp.broadcast_to(m_new[:, :, SM_F + h:SM_F + h + 1], (n_seq, 1, SMALL_W))

    def per_head():
        dmat = dmat_ref[...]

        def per_sequence(b, carry):
            load = {name: functools.partial(tile.seq, ref, b) for name, ref in blocks.items()}
            _sequence_chunk(load, prm, dmat, c, _state_io(b, in_states, out_states), emit_for(b))
            return carry

        lax.fori_loop(0, n_seq, per_sequence, 0)

    lax.cond(worst[0, 0, 0] < 1.0, factored, per_head)

    norms = (prm["hg_norm"], prm["gla_norm"], prm["ml_norm"])
    for branch in range(N_BRANCH):
        for h in range(HEADS):
            c0 = branch * BRANCH_W + h * DV
            tile.store(o_ref, c0, c0 + DV,
                       _finish(branch, tile.load(o_ref, c0, c0 + DV), norms[branch], _gate_pre(col, branch, h)))


def _scan_scratch(tile):
    def gated_linear(dk):
        return ([pltpu.VMEM(tile.shape(HEADS * dk), BF16)] * 4
                + [pltpu.VMEM(tile.shape(HEADS * DV), BF16), pltpu.VMEM((tile.n_seq, HEADS * dk, 1), F32)])

    return (gated_linear(HG_DK) + gated_linear(GLA_DK) + [pltpu.VMEM(tile.shape(HEADS * ML_DK), BF16)] * 5
            + [pltpu.VMEM(tile.shape(SMALL_W), F32), pltpu.VMEM((tile.n_seq, 1, SMALL_W), F32)])


def _param_specs(layer, depth):
    return [
        pl.BlockSpec((depth, HEADS * HG_DK), lambda *_: (0, 0)),
        pl.BlockSpec((None, 1, DV), lambda *_: (layer, 0, 0)),
        pl.BlockSpec((None, 1, DV), lambda *_: (layer, 0, 0)),
        pl.BlockSpec((None, 1, SMALL_W), lambda *_: (layer, 0, 0)),
        pl.BlockSpec((None, 1, DV), lambda *_: (layer, 0, 0)),
    ]


BLOCK_COLS = ((2048, COL_HG), (2048, COL_ML), (512, COL_GLA_QK), (512, COL_GLA_V), (512, COL_GLA_G),
              (SMALL_W, COL_SMALL), (HEADS * GLA_DK, 0))


def _prompt_scan(p_all, log_a, params, dmat, layer, depth, batch, n_chunks):
    tile = _Tile(batch, CHUNK, True, batch)
    rows = batch * CHUNK
    full = lambda shape: pl.BlockSpec(shape, lambda c: (0,) * len(shape))
    state_shapes = [(batch, HEADS) + t for t in STATE_TAILS]
    return pl.pallas_call(
        functools.partial(_scan_body, layer=layer, tile=tile, zero_init=True),
        grid=(n_chunks,),
        in_specs=[pl.BlockSpec((rows, width), lambda c, blk=start // width: (c, blk)) for width, start in BLOCK_COLS]
        + _param_specs(layer, depth) + [full(dmat.shape)],
        out_specs=[pl.BlockSpec((rows, N_BRANCH * BRANCH_W), lambda c: (c, 0))] + [full(s) for s in state_shapes],
        out_shape=[jax.ShapeDtypeStruct((n_chunks * rows, N_BRANCH * BRANCH_W), F32)]
        + [jax.ShapeDtypeStruct(s, F32) for s in state_shapes],
        scratch_shapes=_scan_scratch(tile),
        compiler_params=_cparams(("arbitrary",)),
        name="prompt_scan",
    )(*([p_all] * (N_BLOCKS - 1)), log_a, *params, dmat)


N_STACKED = 3


def _sample_scan(p_s, log_a, params, dmat, states, stacks, layer, depth):
    n_seq, seq, _ = p_s.shape
    nb = SAMPLE_NB
    tile = _Tile(nb, seq, False, 2)
    full = lambda shape: pl.BlockSpec(shape, lambda i: (0,) * len(shape))
    layer_block = lambda t: pl.BlockSpec((None, nb, HEADS) + t, lambda i: (layer, i, 0, 0, 0))
    first_stack = N_BLOCKS + N_PARAMS + 1 + len(STATE_TAILS)
    return pl.pallas_call(
        functools.partial(_scan_body, layer=layer, tile=tile, zero_init=False, n_stacked=N_STACKED),
        grid=(n_seq // nb,),
        in_specs=[pl.BlockSpec((nb, seq, width), lambda i, blk=start // width: (i, 0, blk)) for width, start in BLOCK_COLS]
        + _param_specs(layer, depth) + [full(dmat.shape)] + [layer_block(t) for t in STATE_TAILS]
        + [pl.BlockSpec(memory_space=pl.ANY)] * N_STACKED,
        out_specs=[pl.BlockSpec((nb, seq, N_BRANCH * BRANCH_W), lambda i: (i, 0, 0))]
        + [layer_block(t) for t in STATE_TAILS[:N_STACKED]]
        + [pl.BlockSpec((nb, HEADS) + t, lambda i: (i, 0, 0, 0)) for t in STATE_TAILS[N_STACKED:]],
        out_shape=[jax.ShapeDtypeStruct((n_seq, seq, N_BRANCH * BRANCH_W), F32)]
        + [jax.ShapeDtypeStruct(s.shape, F32) for s in stacks]
        + [jax.ShapeDtypeStruct((n_seq, HEADS) + t, F32) for t in STATE_TAILS[N_STACKED:]],
        input_output_aliases={first_stack + k: 1 + k for k in range(N_STACKED)},
        scratch_shapes=_scan_scratch(tile),
        compiler_params=_cparams(("arbitrary",)),
        name="sample_scan",
    )(*([p_s] * (N_BLOCKS - 1)), log_a, *params, dmat, *states, *stacks)


W_IN_MOVES = ((COL_HG, 0, 2048), (COL_ML, 3600, 2048), (COL_GATE, 5656, 3072), (COL_GLA_QK, 2048, 1536),
              (COL_SMALL, 3584, GLA_RANK), (COL_SMALL + GLA_RANK, 5648, 2 * HEADS))
W_IN_USED = COL_SMALL + GLA_RANK + 2 * HEADS
REGROUP_ROWS = 256


def _regroup_body(w_ref, o_ref):
    for dst, src, n in W_IN_MOVES:
        o_ref[:, dst:dst + n] = w_ref[:, src:src + n].astype(BF16)
    o_ref[:, W_IN_USED:] = jnp.zeros((o_ref.shape[0], P_COLS - W_IN_USED), BF16)


def _regroup_w_in(w_in):
    depth, d, d_in = w_in.shape
    assert d_in == sum(n for _, _, n in W_IN_MOVES) and d % REGROUP_ROWS == 0
    return pl.pallas_call(
        _regroup_body,
        grid=(depth, d // REGROUP_ROWS),
        in_specs=[pl.BlockSpec((None, REGROUP_ROWS, d_in), lambda l, i: (l, i, 0))],
        out_specs=pl.BlockSpec((None, REGROUP_ROWS, P_COLS), lambda l, i: (l, i, 0)),
        out_shape=jax.ShapeDtypeStruct((depth, d, P_COLS), BF16),
        compiler_params=_cparams(("arbitrary", "arbitrary")),
        name="regroup_w_in",
    )(w_in)


def kernel(x_prompt, x_sample, state_hgrn, state_gla, state_mlstm_C, state_mlstm_n, state_mlstm_m,
           ffn1_norm, ffn1_w_up, ffn1_w_down, mix_norm, w_in, hgrn_lb_raw, hgrn_out_norm,
           gla_w_gate_lr, gla_b_gate, gla_out_norm, mlstm_b_i, mlstm_b_f, mlstm_out_norm,
           w_branch, w_out, ffn2_norm, ffn2_w_up, ffn2_w_down, final_norm):
    depth = w_in.shape[0]
    batch, seq, _ = x_prompt.shape
    n_seq, dec_seq, _ = x_sample.shape
    assert seq % CHUNK == 0 and dec_seq % CHUNK != 0 and dec_seq & (dec_seq - 1) == 0
    assert (batch * CHUNK) % TM_TOK == 0 and n_seq * dec_seq == TM_TOK and n_seq % SAMPLE_NB == 0
    n_chunks = seq // CHUNK
    n_prompt = batch * seq

    assert TM_FFN % CHUNK == 0 and batch % (TM_FFN // CHUNK) == 0 and (n_seq * dec_seq) % TM_FFN == 0
    x = (x_prompt.reshape(batch, n_chunks, CHUNK, D_MODEL), x_sample.reshape(n_seq * dec_seq, D_MODEL))

    row3 = lambda a: a.reshape(a.shape[0], 1, a.shape[-1])
    w_all = _regroup_w_in(w_in)
    wlr_pad = jnp.pad(gla_w_gate_lr, ((0, 0), (0, SMALL_W - GLA_RANK), (0, 0))).astype(BF16)
    ml_bias = jnp.pad(jnp.concatenate([mlstm_b_i, mlstm_b_f], axis=-1),
                      ((0, 0), (SM_I, SMALL_W - SM_I - 2 * HEADS)))
    scan_params = (hgrn_lb_raw, row3(hgrn_out_norm), row3(gla_out_norm), row3(ml_bias), row3(mlstm_out_norm))
    gla_b3 = row3(gla_b_gate)
    dmat_p = jnp.asarray(_decay_matrix(CHUNK), BF16)
    dmat_s = jnp.asarray(_decay_matrix(dec_seq), BF16)
    sample_states = (state_hgrn, state_gla, state_mlstm_C,
                     state_mlstm_n.reshape(depth, n_seq, HEADS, 1, ML_DK),
                     jnp.broadcast_to(state_mlstm_m[..., None, None], (depth, n_seq, HEADS, 1, SMALL_W)))
    ffn_w = [(row3(ffn1_norm), ffn1_w_up.astype(BF16), ffn1_w_down.astype(BF16)),
             (row3(ffn2_norm), ffn2_w_up.astype(BF16), ffn2_w_down.astype(BF16))]
    w_branch_b, w_out_b, mix_norm3 = w_branch.astype(BF16), w_out.astype(BF16), row3(mix_norm)
    fin = final_norm.reshape(1, D_MODEL)

    p_states, s_small = [], []
    s_stacks = [jnp.zeros((depth, n_seq, HEADS) + t, F32) for t in STATE_TAILS[:N_STACKED]]
    for l in range(depth):
        x = _ffn(x, *ffn_w[0], fin, l, False, (batch, n_chunks), split_in=l == 0)
        p_all = _inproj(x, mix_norm3, w_all, l)
        log_a = _gla_gate(p_all, wlr_pad, gla_b3, l)
        o_p, *ps = _prompt_scan(p_all, log_a, scan_params, dmat_p, l, depth, batch, n_chunks)
        p_s = p_all[n_prompt:].reshape(n_seq, dec_seq, P_COLS)
        log_a_s = log_a[n_prompt:].reshape(n_seq, dec_seq, HEADS * GLA_DK)
        o_s, *ss = _sample_scan(p_s, log_a_s, scan_params, dmat_s, sample_states, s_stacks, l, depth)
        s_stacks = ss[:N_STACKED]
        x = _merge(x, o_p, o_s.reshape(n_seq * dec_seq, N_BRANCH * BRANCH_W), p_all, w_branch_b, w_out_b, l)
        last = l == depth - 1
        x = _ffn(x, *ffn_w[1], fin, l, last, (batch, n_chunks), split_out=last)
        p_states.append(ps)
        s_small.append(ss[N_STACKED:])

    y_prompt = x[0].reshape(batch, seq, D_MODEL)
    y_sample = x[1].reshape(n_seq, dec_seq, D_MODEL)

    stack = lambda states, i: jnp.stack([st[i] for st in states])
    vectors = lambda mn, mm: (mn[..., 0, :], mm[..., 0, 0])
    prompt_out = tuple(stack(p_states, i) for i in range(N_STACKED)) + vectors(stack(p_states, 3), stack(p_states, 4))
    sample_out = tuple(s_stacks) + vectors(stack(s_small, 0), stack(s_small, 1))
    return (y_prompt, y_sample) + prompt_out + sample_out
```

```python
import functools
from typing import NamedTuple

import numpy as np
import jax
import jax.numpy as jnp
from jax import lax
from jax.experimental import pallas as pl
from jax.experimental.pallas import tpu as pltpu

D_MODEL = 1024
HEADS = 4
HG_DK = 128
GLA_DK = 64
GLA_RANK = 16
GLA_GATE_NORM = 16.0
ML_DK = 128
DV = 128
BRANCH_W = 512
N_BRANCH = 3
D_FF = 2816
CHUNK = 64
EPS = 1e-6
NEG_BIG = -1e30

F32 = jnp.float32
BF16 = jnp.bfloat16

COL_HG = 0
COL_ML = 2048
COL_GATE = 4096
COL_GLA_QK = 7168
COL_GLA_V = 7680
COL_GLA_G = 8192
COL_SMALL = 8704
P_COLS = 8832
SMALL_W = 128
SM_I = GLA_RANK
SM_F = GLA_RANK + HEADS

TM_FFN = 512
TM_TOK = 512
PROJ_COL_TILE = 2944
GATE_MAX_TILES = 11
SAMPLE_NB = 8
VMEM_LIMIT = 56 * 1024 * 1024


def _cparams(sem):
    return pltpu.CompilerParams(dimension_semantics=sem, vmem_limit_bytes=VMEM_LIMIT)


def _dot(a, b):
    return jnp.dot(a, b, preferred_element_type=F32)


def _dot_nt(a, b):
    return lax.dot_general(a, b, (((1,), (1,)), ((), ())), preferred_element_type=F32)


def _dot_tn(a, b):
    return lax.dot_general(a, b, (((0,), (0,)), ((), ())), preferred_element_type=F32)


def _rms(x, g):
    return x * lax.rsqrt(jnp.mean(x * x, axis=-1, keepdims=True) + EPS) * g


def _log_sigmoid(x):
    return jnp.minimum(x, 0.0) - jnp.log1p(jnp.exp(-jnp.abs(x)))


def _silu(x):
    return x * jax.nn.sigmoid(x)


def _exact_dot(m_bf16, x):
    hi = x.astype(BF16)
    r1 = x - hi.astype(F32)
    mid = r1.astype(BF16)
    lo = (r1 - mid.astype(F32)).astype(BF16)
    return _dot(m_bf16, hi) + _dot(m_bf16, mid) + _dot(m_bf16, lo)


def _level_sizes(c):
    out, m = [], c // 2
    while m >= 1:
        out.append(m)
        m //= 2
    return out


def _decay_matrix(c):
    blocks = []
    for m in _level_sizes(c):
        mat = np.zeros((c, c), np.float32)
        for t in range(c):
            mid = (t // (2 * m)) * (2 * m) + m
            if t >= mid:
                mat[t, mid:t + 1] = 1.0
            else:
                mat[t, t + 1:mid] = 1.0
        blocks.append(mat)
    blocks.append(np.tril(np.ones((c, c), np.float32)))
    blocks.append(np.triu(np.ones((c, c), np.float32), 1))
    return np.concatenate(blocks, axis=0)


def _eye(n):
    return lax.broadcasted_iota(jnp.int32, (n, n), 0) == lax.broadcasted_iota(jnp.int32, (n, n), 1)


def _pair_masks(c):
    ti = lax.broadcasted_iota(jnp.int32, (c, c), 0)
    si = lax.broadcasted_iota(jnp.int32, (c, c), 1)
    levels = []
    for m in _level_sizes(c):
        same = (ti // (2 * m)) == (si // (2 * m))
        levels.append(same & ((ti & m) != 0) & ((si & m) == 0))
    return levels, ti == si, si <= ti


def _column_of(row, eye):
    return jnp.sum(jnp.where(eye, row, 0.0), axis=1, keepdims=True)


def _row_of(col, eye):
    return jnp.sum(jnp.where(eye, col, 0.0), axis=0, keepdims=True)


def _ffn_body(*refs, final, n_prompt_tiles, split_in, split_out):
    refs = list(refs)
    x_refs = [refs.pop(0) for _ in range(2 if split_in else 1)]
    g_ref, wup_ref, wdn_ref, fin_ref, *o_refs = refs
    is_prompt = pl.program_id(0) < n_prompt_tiles
    if split_in:
        x = jnp.where(is_prompt, x_refs[0][...].reshape(TM_FFN, D_MODEL), x_refs[1][...])
    else:
        x = x_refs[0][...]
    h = _rms(x, g_ref[...]).astype(BF16)
    gu = _dot(h, wup_ref[...])
    act = _silu(gu[:, :D_FF]) * gu[:, D_FF:]
    out = x + 0.5 * _dot(act.astype(BF16), wdn_ref[...])
    if final:
        out = _rms(out, fin_ref[...])
    if split_out:
        @pl.when(is_prompt)
        def _():
            o_refs[0][...] = out.reshape(o_refs[0].shape)

        @pl.when(jnp.logical_not(is_prompt))
        def _():
            o_refs[1][...] = out
    else:
        o_refs[0][...] = out


def _ffn(x, norm, w_up, w_down, final_norm, layer, final, prompt_shape, split_in=False, split_out=False):
    batch, n_chunks = prompt_shape
    n_prompt_tiles = batch * n_chunks * CHUNK // TM_FFN
    seqs_per_tile = TM_FFN // CHUNK
    tiles_per_chunk = batch // seqs_per_tile

    def prompt_idx(i):
        j = jnp.minimum(i, n_prompt_tiles - 1)
        return (j % tiles_per_chunk, j // tiles_per_chunk, 0, 0)

    split_specs = [pl.BlockSpec((seqs_per_tile, None, CHUNK, D_MODEL), prompt_idx),
                   pl.BlockSpec((TM_FFN, D_MODEL), lambda i: (jnp.maximum(i - n_prompt_tiles, 0), 0))]
    joined_spec = pl.BlockSpec((TM_FFN, D_MODEL), lambda i: (i, 0))
    xs = tuple(x) if split_in else (x,)
    n_sample = xs[1].shape[0] if split_in else x.shape[0] - n_prompt_tiles * TM_FFN
    t = n_prompt_tiles * TM_FFN + n_sample
    split_shapes = [jax.ShapeDtypeStruct((batch, n_chunks, CHUNK, D_MODEL), F32),
                    jax.ShapeDtypeStruct((n_sample, D_MODEL), F32)]
    return pl.pallas_call(
        functools.partial(_ffn_body, final=final, n_prompt_tiles=n_prompt_tiles, split_in=split_in,
                          split_out=split_out),
        grid=(t // TM_FFN,),
        in_specs=(split_specs if split_in else [joined_spec]) + [
            pl.BlockSpec((None, 1, D_MODEL), lambda i: (layer, 0, 0)),
            pl.BlockSpec((None, D_MODEL, 2 * D_FF), lambda i: (layer, 0, 0), pipeline_mode=pl.Buffered(1)),
            pl.BlockSpec((None, D_FF, D_MODEL), lambda i: (layer, 0, 0), pipeline_mode=pl.Buffered(1)),
            pl.BlockSpec((1, D_MODEL), lambda i: (0, 0)),
        ],
        out_specs=split_specs if split_out else joined_spec,
        out_shape=split_shapes if split_out else jax.ShapeDtypeStruct((t, D_MODEL), F32),
        compiler_params=_cparams(("arbitrary",)),
        name="ffn",
    )(*xs, norm, w_up, w_down, final_norm)


def _inproj_body(x_ref, g_ref, w_ref, o_ref):
    h = _rms(x_ref[...], g_ref[...]).astype(BF16)
    o_ref[...] = _dot(h, w_ref[...])


def _inproj(x, norm, w_all, layer):
    t = x.shape[0]
    return pl.pallas_call(
        _inproj_body,
        grid=(P_COLS // PROJ_COL_TILE, t // TM_TOK),
        in_specs=[
            pl.BlockSpec((TM_TOK, D_MODEL), lambda j, i: (i, 0)),
            pl.BlockSpec((None, 1, D_MODEL), lambda j, i: (layer, 0, 0)),
            pl.BlockSpec((None, D_MODEL, PROJ_COL_TILE), lambda j, i: (layer, 0, j)),
        ],
        out_specs=pl.BlockSpec((TM_TOK, PROJ_COL_TILE), lambda j, i: (i, j)),
        out_shape=jax.ShapeDtypeStruct((t, P_COLS), F32),
        compiler_params=_cparams(("arbitrary", "arbitrary")),
        name="inproj",
    )(x, norm, w_all)


def _merge_body(x_ref, op_ref, os_ref, g0_ref, g1_ref, g2_ref, wb_ref, wo_ref, o_ref, *, n_prompt_tiles):
    is_prompt = pl.program_id(0) < n_prompt_tiles
    merged = None
    for c, g_ref in enumerate((g0_ref, g1_ref, g2_ref)):
        cs = slice(c * BRANCH_W, (c + 1) * BRANCH_W)
        br = jnp.where(is_prompt, op_ref[:, cs], os_ref[:, cs])
        term = jax.nn.sigmoid(g_ref[...]) * _dot(br, wb_ref[c])
        merged = term if merged is None else merged + term
    o_ref[...] = x_ref[...] + _dot(merged.astype(BF16), wo_ref[...])


def _merge(x, o_prompt, o_sample, p_all, w_branch, w_out, layer):
    t = x.shape[0]
    n_prompt_tiles = o_prompt.shape[0] // TM_TOK
    gate_blk = COL_GATE // D_MODEL

    def gate_spec(c):
        return pl.BlockSpec((TM_TOK, D_MODEL), lambda i: (i, gate_blk + c))

    return pl.pallas_call(
        functools.partial(_merge_body, n_prompt_tiles=n_prompt_tiles),
        grid=(t // TM_TOK,),
        in_specs=[
            pl.BlockSpec((TM_TOK, D_MODEL), lambda i: (i, 0)),
            pl.BlockSpec((TM_TOK, N_BRANCH * BRANCH_W), lambda i: (jnp.minimum(i, n_prompt_tiles - 1), 0)),
            pl.BlockSpec((TM_TOK, N_BRANCH * BRANCH_W), lambda i: (0, 0)),
            gate_spec(0), gate_spec(1), gate_spec(2),
            pl.BlockSpec((None, N_BRANCH, BRANCH_W, D_MODEL), lambda i: (layer, 0, 0, 0)),
            pl.BlockSpec((None, D_MODEL, D_MODEL), lambda i: (layer, 0, 0)),
        ],
        out_specs=pl.BlockSpec((TM_TOK, D_MODEL), lambda i: (i, 0)),
        out_shape=jax.ShapeDtypeStruct((t, D_MODEL), F32),
        compiler_params=_cparams(("arbitrary",)),
        name="merge",
    )(x, o_prompt, o_sample, p_all, p_all, p_all, w_branch, w_out)


def _gla_gate_body(sm_ref, wlr_ref, b_ref, o_ref):
    z = _dot(sm_ref[...].astype(BF16), wlr_ref[...]) + b_ref[...]
    o_ref[...] = _log_sigmoid(z) / GLA_GATE_NORM


def _gla_gate(p_all, wlr_pad, gla_b, layer):
    t = p_all.shape[0]
    n = HEADS * GLA_DK
    tm = TM_TOK * max(d for d in range(1, GATE_MAX_TILES + 1) if (t // TM_TOK) % d == 0)
    return pl.pallas_call(
        _gla_gate_body,
        grid=(t // tm,),
        in_specs=[
            pl.BlockSpec((tm, SMALL_W), lambda i: (i, COL_SMALL // SMALL_W)),
            pl.BlockSpec((None, SMALL_W, n), lambda i: (layer, 0, 0)),
            pl.BlockSpec((None, 1, n), lambda i: (layer, 0, 0)),
        ],
        out_specs=pl.BlockSpec((tm, n), lambda i: (i, 0)),
        out_shape=jax.ShapeDtypeStruct((t, n), F32),
        compiler_params=_cparams(("arbitrary",)),
        name="gla_gate",
    )(p_all, wlr_pad, gla_b)


def _layer_lower_bound(lb_raw, layer):
    e = jnp.exp(lb_raw - jnp.max(lb_raw, axis=0, keepdims=True))
    soft = e / jnp.sum(e, axis=0, keepdims=True)
    lb = jnp.zeros_like(soft[0:1])
    for j in range(1, layer + 1):
        lb = lb + soft[j:j + 1]
    return lb


def _head_norm(o, g):
    return o * lax.rsqrt(jnp.mean(o * o, axis=-1, keepdims=True) + EPS) * g


def _gated_linear_branch(q_of, k_of, v_of, logf, dk, dmat, masks, read_state, write_state, emit):
    level_masks, eye, _ = masks
    n_lev = len(level_masks)
    c = eye.shape[0]
    eye_dk = _eye(dk)
    e_all = _exact_dot(dmat, logf)
    for h in range(HEADS):
        ks = slice(h * dk, (h + 1) * dk)
        qh, kh, vh = q_of(h), k_of(h), v_of(h).astype(BF16)
        att = jnp.where(eye, _dot_nt(qh.astype(BF16), kh.astype(BF16)), 0.0)
        for lv in range(n_lev):
            a = jnp.exp(e_all[lv * c:(lv + 1) * c, ks])
            att = att + jnp.where(level_masks[lv], _dot_nt((qh * a).astype(BF16), (kh * a).astype(BF16)), 0.0)
        cum = e_all[n_lev * c:(n_lev + 1) * c, ks]
        rev = e_all[(n_lev + 1) * c:(n_lev + 2) * c, ks]
        st = read_state(h)
        emit(h, _dot(att.astype(BF16), vh) + _dot((qh * jnp.exp(cum)).astype(BF16), st.astype(BF16)))
        decay = _column_of(jnp.exp(cum[c - 1:c, :]), eye_dk)
        write_state(h, st * decay + _dot_tn((kh * jnp.exp(rev)).astype(BF16), vh))


def _mlstm_branch(q_of, k_of, v_of, gates, bcum, masks, read_state, write_state, emit):
    _, eye, tri = masks
    c = eye.shape[0]
    for h in range(HEADS):
        qh, kh, vh = q_of(h), k_of(h), v_of(h).astype(BF16)
        qb = qh.astype(BF16)
        bcol = bcum[:, SM_F + h:SM_F + h + 1]
        icol = gates[:, SM_I + h:SM_I + h + 1]
        log_d = jnp.where(tri, bcol + _row_of(icol - bcol, eye), NEG_BIG)
        cst, nrow, m_prev = read_state(h)
        inter = bcol + m_prev
        m_t = jnp.maximum(inter, jnp.max(log_d, axis=1, keepdims=True))
        d = jnp.exp(log_d - m_t)
        w_inter = jnp.exp(inter - m_t)
        qk = _dot_nt(qb, kh.astype(BF16)) * d
        num = _dot(qk.astype(BF16), vh) + w_inter * _dot(qb, cst.astype(BF16))
        den = jnp.sum(qk, axis=1, keepdims=True) + w_inter * jnp.sum(qh * nrow, axis=1, keepdims=True)
        emit(h, num / jnp.maximum(jnp.abs(den), jnp.exp(-m_t)))
        m_new = m_t[c - 1:c, :]
        b_end = bcol[c - 1:c, :]
        kw = jnp.exp(b_end - bcol + icol - m_new) * kh
        carry = jnp.exp(b_end + m_prev - m_new)
        write_state(h,
                    carry * cst + _dot_tn(kw.astype(BF16), vh),
                    carry * nrow + jnp.sum(kw, axis=0, keepdims=True),
                    m_new)


def _hgrn_log_decay(zf, lb):
    return jnp.log(lb + (1.0 - lb) * jax.nn.sigmoid(zf))


def _mlstm_gates(p_small, ml_bias):
    lane = lax.broadcasted_iota(jnp.int32, p_small.shape, p_small.ndim - 1)
    biased = p_small + ml_bias
    return jnp.where(lane >= SM_F, _log_sigmoid(biased), biased)


def _finish(branch, o, norm_g, gate_pre):
    return _head_norm(o, norm_g) * (jax.nn.sigmoid(gate_pre) if branch == 2 else _silu(gate_pre))


def _gate_pre(load, branch, h):
    name, c0 = (("hg", 3 * HEADS * HG_DK), ("gg", 0), ("ml", 3 * HEADS * ML_DK))[branch]
    return load[name](c0 + h * DV, c0 + (h + 1) * DV)


def _sequence_chunk(load, prm, dmat, c, state_io, emit):
    hg_io, gla_io, ml_io = state_io
    lb = prm["lb"]
    masks = _pair_masks(c)
    n_lev = len(masks[0])
    w, wg = HEADS * HG_DK, HEADS * GLA_DK
    head = lambda h, width=DV: (h * width, (h + 1) * width)
    hg, gqk, ml = load["hg"], load["gqk"], load["ml"]

    _gated_linear_branch(
        lambda h: _silu(hg(*head(h))),
        lambda h: (1.0 - lb[:, slice(*head(h))]) * jax.nn.sigmoid(-hg(w + h * HG_DK, w + (h + 1) * HG_DK)),
        lambda h: hg(2 * w + h * DV, 2 * w + (h + 1) * DV),
        _hgrn_log_decay(hg(w, 2 * w), lb),
        HG_DK, dmat, masks, hg_io[0], hg_io[1], functools.partial(emit, 0))

    _gated_linear_branch(
        lambda h: gqk(*head(h, GLA_DK)) * (GLA_DK ** -0.5),
        lambda h: gqk(wg + h * GLA_DK, wg + (h + 1) * GLA_DK),
        lambda h: load["gv"](*head(h)),
        load["ga"](0, wg),
        GLA_DK, dmat, masks, gla_io[0], gla_io[1], functools.partial(emit, 1))

    gates = _mlstm_gates(load["sm"](0, SMALL_W), prm["ml_bias"])
    _mlstm_branch(
        lambda h: ml(*head(h)),
        lambda h: ml(w + h * ML_DK, w + (h + 1) * ML_DK) * (ML_DK ** -0.5),
        lambda h: ml(2 * w + h * DV, 2 * w + (h + 1) * DV),
        gates, _exact_dot(dmat[n_lev * c:(n_lev + 1) * c, :], gates), masks, ml_io[0], ml_io[1],
        functools.partial(emit, 2))


def _state_io(b, in_refs, out_refs):
    ihg, igla, imc, imn, imm = in_refs
    ohg, ogla, omc, omn, omm = out_refs

    def write_to(ref):
        def write(h, s):
            ref[b, h] = s
        return write

    def ml_write(h, c_new, n_new, m_new):
        omc[b, h] = c_new
        omn[b, h] = n_new
        omm[b, h] = jnp.broadcast_to(m_new, (1, SMALL_W))

    return ((lambda h: ihg[b, h], write_to(ohg)),
            (lambda h: igla[b, h], write_to(ogla)),
            (lambda h: (imc[b, h], imn[b, h], imm[b, h][:, 0:1]), ml_write))


BLOCK_NAMES = ("hg", "ml", "gqk", "gv", "gg", "sm", "ga")
N_BLOCKS = len(BLOCK_NAMES)
PARAM_NAMES = ("hg_norm", "gla_norm", "ml_bias", "ml_norm")
N_PARAMS = 1 + len(PARAM_NAMES)
STATE_TAILS = [(HG_DK, DV), (GLA_DK, DV), (ML_DK, DV), (1, ML_DK), (1, SMALL_W)]
GL_SCRATCH = ("qs", "ks", "qe", "ke", "v", "dec")
ML_SCRATCH = ("qm", "km", "qw", "kw", "v", "emt", "carry")
SAFE_LOG_RANGE = 80.0
ML_SAFE_RANGE = 40.0


class _Tile(NamedTuple):
    n_seq: int
    c: int
    flat: bool
    unroll: int

    def shape(self, width):
        return (self.n_seq * self.c, width) if self.flat else (self.n_seq, self.c, width)

    def load(self, ref, c0, c1):
        if self.flat:
            return ref[:, c0:c1].reshape(self.n_seq, self.c, c1 - c0)
        return ref[:, :, c0:c1]

    def store(self, ref, c0, c1, x):
        if self.flat:
            ref[:, c0:c1] = x.reshape(self.n_seq * self.c, c1 - c0).astype(ref.dtype)
        else:
            ref[:, :, c0:c1] = x.astype(ref.dtype)

    def rows(self, b):
        return pl.ds(pl.multiple_of(b * self.c, self.c), self.c)

    def seq(self, ref, b, c0, c1):
        return ref[self.rows(b), c0:c1] if self.flat else ref[b, :, c0:c1]

    def seq_store(self, ref, b, c0, c1, x):
        if self.flat:
            ref[self.rows(b), c0:c1] = x
        else:
            ref[b, :, c0:c1] = x

    def scan(self, x, combine, fill):
        n_seq, c, w = x.shape
        if self.flat:
            y = x.reshape(n_seq * c, w)
            pos = lax.broadcasted_iota(jnp.int32, y.shape, 0) % c
            shift = 1
            while shift < c:
                y = combine(y, jnp.where(pos >= shift, pltpu.roll(y, shift, 0), fill))
                shift *= 2
            return y.reshape(n_seq, c, w)
        t = lax.broadcasted_iota(jnp.int32, x.shape, 1)
        acc = jnp.full(x.shape, fill, x.dtype)
        for j in range(c):
            acc = combine(acc, jnp.where(t >= j, x[:, j:j + 1, :], fill))
        return acc


def _max_all(x):
    for axis in (2, 1, 0):
        x = jnp.max(x, axis=axis, keepdims=True)
    return x


def _prepare_gated_linear(tile, q, k, v, logf, sc):
    c, n = tile.c, logf.shape[-1]
    cum = tile.scan(logf, jnp.add, 0.0)
    ref, end = cum[:, c // 2 - 1:c // 2, :], cum[:, c - 1:c, :]
    tile.store(sc["qs"], 0, n, q * jnp.exp(cum - ref))
    tile.store(sc["ks"], 0, n, k * jnp.exp(ref - cum))
    tile.store(sc["qe"], 0, n, q * jnp.exp(cum))
    tile.store(sc["ke"], 0, n, k * jnp.exp(end - cum))
    tile.store(sc["v"], 0, v.shape[-1], v)
    decay = jnp.transpose(jnp.exp(cum[:, c - 1, :]))
    for b in range(tile.n_seq):
        sc["dec"][b] = decay[:, b:b + 1]
    return _max_all(jnp.abs(cum - ref))


def _prepare_mlstm(tile, q, k, v, gates, m_state, sc):
    c = tile.c
    lane = lax.broadcasted_iota(jnp.int32, gates.shape, 2)
    g = jnp.where((lane >= SM_I) & (lane < SM_F + HEADS), gates, 0.0)
    bcum = tile.scan(g, jnp.add, 0.0)
    a = pltpu.roll(g, HEADS, 2) - bcum
    lane1 = lax.broadcasted_iota(jnp.int32, (tile.n_seq, 1, SMALL_W), 2)
    m_prev = jnp.zeros((tile.n_seq, 1, SMALL_W), F32)
    for h in range(HEADS):
        m_prev = jnp.where(lane1 == SM_F + h, m_state[:, h], m_prev)
    big_m = jnp.maximum(tile.scan(a, jnp.maximum, NEG_BIG), m_prev)
    ref, m_end = big_m[:, c // 2 - 1:c // 2, :], big_m[:, c - 1:c, :]
    scales = {"qm": jnp.exp(ref - big_m), "km": jnp.exp(a - ref),
              "qw": jnp.exp(m_prev - big_m), "kw": jnp.exp(a - m_end)}
    tile.store(sc["emt"], 0, SMALL_W, jnp.exp(-(bcum + big_m)))
    sc["carry"][...] = jnp.exp(m_prev - m_end)
    for h in range(HEADS):
        h0, h1 = h * ML_DK, (h + 1) * ML_DK
        qh, kh = q[:, :, h0:h1], k[:, :, h0:h1] * (ML_DK ** -0.5)
        for name, x in (("qm", qh), ("km", kh), ("qw", qh), ("kw", kh)):
            tile.store(sc[name], h0, h1, x * scales[name][:, :, SM_F + h:SM_F + h + 1])
    tile.store(sc["v"], 0, v.shape[-1], v)
    spread = jnp.where((lane >= SM_F) & (lane < SM_F + HEADS), jnp.abs(big_m - ref), 0.0)
    return _max_all(spread), bcum[:, c - 1:c, :] + m_end


def _gated_linear_units(tile, b, dk, sc, st_in, st_out, tri, emit):
    for h in range(HEADS):
        k0, k1 = h * dk, (h + 1) * dk
        att = jnp.where(tri, _dot_nt(tile.seq(sc["qs"], b, k0, k1), tile.seq(sc["ks"], b, k0, k1)), 0.0)
        vh = tile.seq(sc["v"], b, h * DV, (h + 1) * DV)
        st = st_in[b, h]
        emit(h, _dot(att.astype(BF16), vh) + _dot(tile.seq(sc["qe"], b, k0, k1), st.astype(BF16)))
        st_out[b, h] = st * sc["dec"][b, k0:k1, :] + _dot_tn(tile.seq(sc["ke"], b, k0, k1), vh)


def _mlstm_units(tile, b, sc, c_in, n_in, c_out, n_out, tri, emit):
    for h in range(HEADS):
        h0, h1 = h * ML_DK, (h + 1) * ML_DK
        att = jnp.where(tri, _dot_nt(tile.seq(sc["qm"], b, h0, h1), tile.seq(sc["km"], b, h0, h1)), 0.0)
        vh, qw, kw = (tile.seq(sc[name], b, h0, h1) for name in ("v", "qw", "kw"))
        cst, nrow = c_in[b, h], n_in[b, h]
        num = _dot(att.astype(BF16), vh) + _dot(qw, cst.astype(BF16))
        den = jnp.sum(att, axis=1, keepdims=True) + jnp.sum(qw.astype(F32) * nrow, axis=1, keepdims=True)
        emit(h, num / jnp.maximum(jnp.abs(den), tile.seq(sc["emt"], b, SM_F + h, SM_F + h + 1)))
        carry = sc["carry"][b][:, SM_F + h:SM_F + h + 1]
        c_out[b, h] = carry * cst + _dot_tn(kw, vh)
        n_out[b, h] = carry * nrow + jnp.sum(kw.astype(F32), axis=0, keepdims=True)


def _scan_body(*refs, layer, tile, zero_init, n_stacked=0):
    refs = list(refs)
    take_n = lambda n: [refs.pop(0) for _ in range(n)]
    blocks = dict(zip(BLOCK_NAMES, take_n(N_BLOCKS)))
    param_refs = take_n(N_PARAMS)
    dmat_ref, = take_n(1)
    in_states = None if zero_init else take_n(len(STATE_TAILS))
    take_n(n_stacked)
    o_ref, = take_n(1)
    out_states = take_n(len(STATE_TAILS))
    hg_sc = dict(zip(GL_SCRATCH, take_n(len(GL_SCRATCH))))
    gla_sc = dict(zip(GL_SCRATCH, take_n(len(GL_SCRATCH))))
    ml_sc = dict(zip(ML_SCRATCH, take_n(len(ML_SCRATCH))))
    raw_ref, = take_n(1)
    c, n_seq = tile.c, tile.n_seq

    if zero_init:
        in_states = out_states

        @pl.when(pl.program_id(0) == 0)
        def _():
            for r in out_states:
                r[...] = jnp.zeros_like(r)

    lb_ref, *rest = param_refs
    prm = dict(zip(PARAM_NAMES, (r[...] for r in rest)), lb=_layer_lower_bound(lb_ref[...], layer))
    lb = prm["lb"]
    col = {name: functools.partial(tile.load, ref) for name, ref in blocks.items()}
    w, wg = HEADS * HG_DK, HEADS * GLA_DK
    hg, gqk, ml = col["hg"], col["gqk"], col["ml"]

    zf = hg(w, 2 * w)
    gl_spread = jnp.maximum(
        _prepare_gated_linear(tile, _silu(hg(0, w)), (1.0 - lb) * jax.nn.sigmoid(-zf), hg(2 * w, 3 * w),
                              _hgrn_log_decay(zf, lb), hg_sc),
        _prepare_gated_linear(tile, gqk(0, wg) * (GLA_DK ** -0.5), gqk(wg, 2 * wg), col["gv"](0, HEADS * DV),
                              col["ga"](0, wg), gla_sc))
    ml_spread, m_new = _prepare_mlstm(tile, ml(0, w), ml(w, 2 * w), ml(2 * w, 3 * w),
                                      _mlstm_gates(col["sm"](0, SMALL_W), prm["ml_bias"]),
                                      in_states[4][...], ml_sc)
    worst = jnp.maximum(gl_spread * (1.0 / SAFE_LOG_RANGE), ml_spread * (1.0 / ML_SAFE_RANGE))
    tri = _pair_masks(c)[2]

    def emit_for(b):
        def emit(branch, h, val):
            c0 = branch * BRANCH_W + h * DV
            tile.seq_store(raw_ref, b, c0, c0 + DV, val)
        return emit

    def factored():
        def per_sequence(b, carry):
            emit = emit_for(b)
            _gated_linear_units(tile, b, HG_DK, hg_sc, in_states[0], out_states[0], tri, functools.partial(emit, 0))
            _gated_linear_units(tile, b, GLA_DK, gla_sc, in_states[1], out_states[1], tri, functools.partial(emit, 1))
            _mlstm_units(tile, b, ml_sc, in_states[2], in_states[3], out_states[2], out_states[3], tri,
                         functools.partial(emit, 2))
            return carry

        lax.fori_loop(0, n_seq, per_sequence, 0, unroll=tile.unroll)
        for h in range(HEADS):
            out_states[4][:, h] = jnp.broadcast_to(m_new[:, :, SM_F + h:SM_F + h + 1], (n_seq, 1, SMALL_W))

    def per_head():
        dmat = dmat_ref[...]

        def per_sequence(b, carry):
            load = {name: functools.partial(tile.seq, ref, b) for name, ref in blocks.items()}
            _sequence_chunk(load, prm, dmat, c, _state_io(b, in_states, out_states), emit_for(b))
            return carry

        lax.fori_loop(0, n_seq, per_sequence, 0)

    lax.cond(worst[0, 0, 0] < 1.0, factored, per_head)

    norms = (prm["hg_norm"], prm["gla_norm"], prm["ml_norm"])
    for branch in range(N_BRANCH):
        for h in range(HEADS):
            c0 = branch * BRANCH_W + h * DV
            tile.store(o_ref, c0, c0 + DV,
                       _finish(branch, tile.load(raw_ref, c0, c0 + DV), norms[branch], _gate_pre(col, branch, h)))


def _scan_scratch(tile):
    def gated_linear(dk):
        return ([pltpu.VMEM(tile.shape(HEADS * dk), BF16)] * 4
                + [pltpu.VMEM(tile.shape(HEADS * DV), BF16), pltpu.VMEM((tile.n_seq, HEADS * dk, 1), F32)])

    return (gated_linear(HG_DK) + gated_linear(GLA_DK) + [pltpu.VMEM(tile.shape(HEADS * ML_DK), BF16)] * 5
            + [pltpu.VMEM(tile.shape(SMALL_W), F32), pltpu.VMEM((tile.n_seq, 1, SMALL_W), F32),
               pltpu.VMEM(tile.shape(N_BRANCH * BRANCH_W), F32)])


def _param_specs(layer, depth):
    return [
        pl.BlockSpec((depth, HEADS * HG_DK), lambda *_: (0, 0)),
        pl.BlockSpec((None, 1, DV), lambda *_: (layer, 0, 0)),
        pl.BlockSpec((None, 1, DV), lambda *_: (layer, 0, 0)),
        pl.BlockSpec((None, 1, SMALL_W), lambda *_: (layer, 0, 0)),
        pl.BlockSpec((None, 1, DV), lambda *_: (layer, 0, 0)),
    ]


BLOCK_COLS = ((2048, COL_HG), (2048, COL_ML), (512, COL_GLA_QK), (512, COL_GLA_V), (512, COL_GLA_G),
              (SMALL_W, COL_SMALL), (HEADS * GLA_DK, 0))


def _prompt_scan(p_all, log_a, params, dmat, layer, depth, batch, n_chunks):
    tile = _Tile(batch, CHUNK, True, batch)
    rows = batch * CHUNK
    full = lambda shape: pl.BlockSpec(shape, lambda c: (0,) * len(shape))
    state_shapes = [(batch, HEADS) + t for t in STATE_TAILS]
    return pl.pallas_call(
        functools.partial(_scan_body, layer=layer, tile=tile, zero_init=True),
        grid=(n_chunks,),
        in_specs=[pl.BlockSpec((rows, width), lambda c, blk=start // width: (c, blk)) for width, start in BLOCK_COLS]
        + _param_specs(layer, depth) + [full(dmat.shape)],
        out_specs=[pl.BlockSpec((rows, N_BRANCH * BRANCH_W), lambda c: (c, 0))] + [full(s) for s in state_shapes],
        out_shape=[jax.ShapeDtypeStruct((n_chunks * rows, N_BRANCH * BRANCH_W), BF16)]
        + [jax.ShapeDtypeStruct(s, F32) for s in state_shapes],
        scratch_shapes=_scan_scratch(tile),
        compiler_params=_cparams(("arbitrary",)),
        name="prompt_scan",
    )(*([p_all] * (N_BLOCKS - 1)), log_a, *params, dmat)


N_STACKED = 3


def _sample_scan(p_s, log_a, params, dmat, states, stacks, layer, depth):
    n_seq, seq, _ = p_s.shape
    nb = SAMPLE_NB
    tile = _Tile(nb, seq, False, 2)
    full = lambda shape: pl.BlockSpec(shape, lambda i: (0,) * len(shape))
    layer_block = lambda t: pl.BlockSpec((None, nb, HEADS) + t, lambda i: (layer, i, 0, 0, 0))
    first_stack = N_BLOCKS + N_PARAMS + 1 + len(STATE_TAILS)
    return pl.pallas_call(
        functools.partial(_scan_body, layer=layer, tile=tile, zero_init=False, n_stacked=N_STACKED),
        grid=(n_seq // nb,),
        in_specs=[pl.BlockSpec((nb, seq, width), lambda i, blk=start // width: (i, 0, blk)) for width, start in BLOCK_COLS]
        + _param_specs(layer, depth) + [full(dmat.shape)] + [layer_block(t) for t in STATE_TAILS]
        + [pl.BlockSpec(memory_space=pl.ANY)] * N_STACKED,
        out_specs=[pl.BlockSpec((nb, seq, N_BRANCH * BRANCH_W), lambda i: (i, 0, 0))]
        + [layer_block(t) for t in STATE_TAILS[:N_STACKED]]
        + [pl.BlockSpec((nb, HEADS) + t, lambda i: (i, 0, 0, 0)) for t in STATE_TAILS[N_STACKED:]],
        out_shape=[jax.ShapeDtypeStruct((n_seq, seq, N_BRANCH * BRANCH_W), BF16)]
        + [jax.ShapeDtypeStruct(s.shape, F32) for s in stacks]
        + [jax.ShapeDtypeStruct((n_seq, HEADS) + t, F32) for t in STATE_TAILS[N_STACKED:]],
        input_output_aliases={first_stack + k: 1 + k for k in range(N_STACKED)},
        scratch_shapes=_scan_scratch(tile),
        compiler_params=_cparams(("arbitrary",)),
        name="sample_scan",
    )(*([p_s] * (N_BLOCKS - 1)), log_a, *params, dmat, *states, *stacks)


W_IN_MOVES = ((COL_HG, 0, 2048), (COL_ML, 3600, 2048), (COL_GATE, 5656, 3072), (COL_GLA_QK, 2048, 1536))
W_IN_SMALL = ((3584, GLA_RANK), (5648, 2 * HEADS))
REGROUP_ROWS = 256
REGROUP_PIECE = 512


def _regroup_body(wt_ref, o_ref):
    rows = o_ref.shape[0]
    eye = _eye(rows).astype(BF16)
    transposed = lambda piece: _dot_nt(eye, piece.astype(BF16)).astype(BF16)
    for dst, src, n in W_IN_MOVES:
        for c in range(0, n, REGROUP_PIECE):
            m = min(REGROUP_PIECE, n - c)
            o_ref[:, dst + c:dst + c + m] = transposed(wt_ref[src + c:src + c + m, :])
    pieces = [wt_ref[src:src + n, :] for src, n in W_IN_SMALL]
    pad = SMALL_W - sum(n for _, n in W_IN_SMALL)
    o_ref[:, COL_SMALL:] = transposed(jnp.concatenate(pieces + [jnp.zeros((pad, rows), F32)], axis=0))


def _regroup_w_in(w_in):
    depth, d, d_in = w_in.shape
    assert d_in == sum(n for _, _, n in W_IN_MOVES) + sum(n for _, n in W_IN_SMALL) and d % REGROUP_ROWS == 0
    return pl.pallas_call(
        _regroup_body,
        grid=(depth, d // REGROUP_ROWS),
        in_specs=[pl.BlockSpec((None, d_in, REGROUP_ROWS), lambda l, i: (l, 0, i))],
        out_specs=pl.BlockSpec((None, REGROUP_ROWS, P_COLS), lambda l, i: (l, i, 0)),
        out_shape=jax.ShapeDtypeStruct((depth, d, P_COLS), BF16),
        compiler_params=_cparams(("arbitrary", "arbitrary")),
        name="regroup_w_in",
    )(jnp.swapaxes(w_in, 1, 2))


def kernel(x_prompt, x_sample, state_hgrn, state_gla, state_mlstm_C, state_mlstm_n, state_mlstm_m,
           ffn1_norm, ffn1_w_up, ffn1_w_down, mix_norm, w_in, hgrn_lb_raw, hgrn_out_norm,
           gla_w_gate_lr, gla_b_gate, gla_out_norm, mlstm_b_i, mlstm_b_f, mlstm_out_norm,
           w_branch, w_out, ffn2_norm, ffn2_w_up, ffn2_w_down, final_norm):
    depth = w_in.shape[0]
    batch, seq, _ = x_prompt.shape
    n_seq, dec_seq, _ = x_sample.shape
    assert seq % CHUNK == 0 and dec_seq % CHUNK != 0 and dec_seq & (dec_seq - 1) == 0
    assert (batch * CHUNK) % TM_TOK == 0 and n_seq * dec_seq == TM_TOK and n_seq % SAMPLE_NB == 0
    n_chunks = seq // CHUNK
    n_prompt = batch * seq

    assert TM_FFN % CHUNK == 0 and batch % (TM_FFN // CHUNK) == 0 and (n_seq * dec_seq) % TM_FFN == 0
    x = (x_prompt.reshape(batch, n_chunks, CHUNK, D_MODEL), x_sample.reshape(n_seq * dec_seq, D_MODEL))

    row3 = lambda a: a.reshape(a.shape[0], 1, a.shape[-1])
    w_all = _regroup_w_in(w_in)
    wlr_pad = jnp.pad(gla_w_gate_lr, ((0, 0), (0, SMALL_W - GLA_RANK), (0, 0))).astype(BF16)
    ml_bias = jnp.pad(jnp.concatenate([mlstm_b_i, mlstm_b_f], axis=-1),
                      ((0, 0), (SM_I, SMALL_W - SM_I - 2 * HEADS)))
    scan_params = (hgrn_lb_raw, row3(hgrn_out_norm), row3(gla_out_norm), row3(ml_bias), row3(mlstm_out_norm))
    gla_b3 = row3(gla_b_gate)
    dmat_p = jnp.asarray(_decay_matrix(CHUNK), BF16)
    dmat_s = jnp.asarray(_decay_matrix(dec_seq), BF16)
    sample_states = (state_hgrn, state_gla, state_mlstm_C,
                     state_mlstm_n.reshape(depth, n_seq, HEADS, 1, ML_DK),
                     jnp.broadcast_to(state_mlstm_m[..., None, None], (depth, n_seq, HEADS, 1, SMALL_W)))
    ffn_w = [(row3(ffn1_norm), ffn1_w_up.astype(BF16), ffn1_w_down.astype(BF16)),
             (row3(ffn2_norm), ffn2_w_up.astype(BF16), ffn2_w_down.astype(BF16))]
    w_branch_b, w_out_b, mix_norm3 = w_branch.astype(BF16), w_out.astype(BF16), row3(mix_norm)
    fin = final_norm.reshape(1, D_MODEL)

    p_states, s_small = [], []
    s_stacks = [jnp.zeros((depth, n_seq, HEADS) + t, F32) for t in STATE_TAILS[:N_STACKED]]
    for l in range(depth):
        x = _ffn(x, *ffn_w[0], fin, l, False, (batch, n_chunks), split_in=l == 0)
        p_all = _inproj(x, mix_norm3, w_all, l)
        log_a = _gla_gate(p_all, wlr_pad, gla_b3, l)
        o_p, *ps = _prompt_scan(p_all, log_a, scan_params, dmat_p, l, depth, batch, n_chunks)
        p_s = p_all[n_prompt:].reshape(n_seq, dec_seq, P_COLS)
        log_a_s = log_a[n_prompt:].reshape(n_seq, dec_seq, HEADS * GLA_DK)
        o_s, *ss = _sample_scan(p_s, log_a_s, scan_params, dmat_s, sample_states, s_stacks, l, depth)
        s_stacks = ss[:N_STACKED]
        x = _merge(x, o_p, o_s.reshape(n_seq * dec_seq, N_BRANCH * BRANCH_W), p_all, w_branch_b, w_out_b, l)
        last = l == depth - 1
        x = _ffn(x, *ffn_w[1], fin, l, last, (batch, n_chunks), split_out=last)
        p_states.append(ps)
        s_small.append(ss[N_STACKED:])

    y_prompt = x[0].reshape(batch, seq, D_MODEL)
    y_sample = x[1].reshape(n_seq, dec_seq, D_MODEL)

    stack = lambda states, i: jnp.stack([st[i] for st in states])
    vectors = lambda mn, mm: (mn[..., 0, :], mm[..., 0, 0])
    prompt_out = tuple(stack(p_states, i) for i in range(N_STACKED)) + vectors(stack(p_states, 3), stack(p_states, 4))
    sample_out = tuple(s_stacks) + vectors(stack(s_small, 0), stack(s_small, 1))
    return (y_prompt, y_sample) + prompt_out + sample_out
```

```python
import functools
from typing import NamedTuple

import numpy as np
import jax
import jax.numpy as jnp
from jax import lax
from jax.experimental import pallas as pl
from jax.experimental.pallas import tpu as pltpu

D_MODEL = 1024
HEADS = 4
HG_DK = 128
GLA_DK = 64
GLA_RANK = 16
GLA_GATE_NORM = 16.0
ML_DK = 128
DV = 128
BRANCH_W = 512
N_BRANCH = 3
D_FF = 2816
CHUNK = 64
EPS = 1e-6
NEG_BIG = -1e30

F32 = jnp.float32
BF16 = jnp.bfloat16

W_HG = 4 * HEADS * HG_DK
W_ML = 4 * HEADS * ML_DK
W_GATE = N_BRANCH * D_MODEL
W_GLA_QK = 2 * HEADS * GLA_DK
W_GLA = W_GLA_QK + 2 * HEADS * DV
SMALL_W = 128
COL_HG = 0
COL_ML = COL_HG + W_HG
COL_GATE = COL_ML + W_ML
COL_GLA_QK = COL_GATE + W_GATE
COL_GLA_V = COL_GLA_QK + W_GLA_QK
COL_GLA_G = COL_GLA_V + HEADS * DV
COL_SMALL = COL_GLA_QK + W_GLA
P_COLS = COL_SMALL + SMALL_W
SM_I = GLA_RANK
SM_F = GLA_RANK + HEADS

TM_FFN = 512
TM_TOK = 512
PROJ_COL_TILE = P_COLS // 3
GATE_MAX_TILES = 11
SAMPLE_NB = 8
VMEM_LIMIT = 56 * 1024 * 1024


def _cparams(sem):
    return pltpu.CompilerParams(dimension_semantics=sem, vmem_limit_bytes=VMEM_LIMIT)


def _dot(a, b):
    return jnp.dot(a, b, preferred_element_type=F32)


def _dot_nt(a, b):
    return lax.dot_general(a, b, (((1,), (1,)), ((), ())), preferred_element_type=F32)


def _dot_tn(a, b):
    return lax.dot_general(a, b, (((0,), (0,)), ((), ())), preferred_element_type=F32)


def _rms(x, g):
    return x * lax.rsqrt(jnp.mean(x * x, axis=-1, keepdims=True) + EPS) * g


def _log_sigmoid(x):
    return jnp.minimum(x, 0.0) - jnp.log1p(jnp.exp(-jnp.abs(x)))


def _silu(x):
    return x * jax.nn.sigmoid(x)


def _exact_dot(m_bf16, x):
    hi = x.astype(BF16)
    r1 = x - hi.astype(F32)
    mid = r1.astype(BF16)
    lo = (r1 - mid.astype(F32)).astype(BF16)
    return _dot(m_bf16, hi) + _dot(m_bf16, mid) + _dot(m_bf16, lo)


def _level_sizes(c):
    out, m = [], c // 2
    while m >= 1:
        out.append(m)
        m //= 2
    return out


def _decay_matrix(c):
    blocks = []
    for m in _level_sizes(c):
        mat = np.zeros((c, c), np.float32)
        for t in range(c):
            mid = (t // (2 * m)) * (2 * m) + m
            if t >= mid:
                mat[t, mid:t + 1] = 1.0
            else:
                mat[t, t + 1:mid] = 1.0
        blocks.append(mat)
    blocks.append(np.tril(np.ones((c, c), np.float32)))
    blocks.append(np.triu(np.ones((c, c), np.float32), 1))
    return np.concatenate(blocks, axis=0)


def _eye(n):
    return lax.broadcasted_iota(jnp.int32, (n, n), 0) == lax.broadcasted_iota(jnp.int32, (n, n), 1)


def _pair_masks(c):
    ti = lax.broadcasted_iota(jnp.int32, (c, c), 0)
    si = lax.broadcasted_iota(jnp.int32, (c, c), 1)
    levels = []
    for m in _level_sizes(c):
        same = (ti // (2 * m)) == (si // (2 * m))
        levels.append(same & ((ti & m) != 0) & ((si & m) == 0))
    return levels, ti == si, si <= ti


def _column_of(row, eye):
    return jnp.sum(jnp.where(eye, row, 0.0), axis=1, keepdims=True)


def _row_of(col, eye):
    return jnp.sum(jnp.where(eye, col, 0.0), axis=0, keepdims=True)


def _ffn_body(*refs, final, n_prompt_tiles, split_in, split_out):
    refs = list(refs)
    x_refs = [refs.pop(0) for _ in range(2 if split_in else 1)]
    g_ref, wup_ref, wdn_ref, fin_ref, *o_refs = refs
    is_prompt = pl.program_id(0) < n_prompt_tiles
    if split_in:
        x = jnp.where(is_prompt, x_refs[0][...].reshape(TM_FFN, D_MODEL), x_refs[1][...])
    else:
        x = x_refs[0][...]
    h = _rms(x, g_ref[...]).astype(BF16)
    gu = _dot(h, wup_ref[...])
    act = _silu(gu[:, :D_FF]) * gu[:, D_FF:]
    out = x + 0.5 * _dot(act.astype(BF16), wdn_ref[...])
    if final:
        out = _rms(out, fin_ref[...])
    if split_out:
        @pl.when(is_prompt)
        def _():
            o_refs[0][...] = out.reshape(o_refs[0].shape)

        @pl.when(jnp.logical_not(is_prompt))
        def _():
            o_refs[1][...] = out
    else:
        o_refs[0][...] = out


def _ffn(x, norm, w_up, w_down, final_norm, layer, final, prompt_shape, split_in=False, split_out=False):
    batch, n_chunks = prompt_shape
    n_prompt_tiles = batch * n_chunks * CHUNK // TM_FFN
    seqs_per_tile = TM_FFN // CHUNK
    tiles_per_chunk = batch // seqs_per_tile

    def prompt_idx(i):
        j = jnp.minimum(i, n_prompt_tiles - 1)
        return (j % tiles_per_chunk, j // tiles_per_chunk, 0, 0)

    split_specs = [pl.BlockSpec((seqs_per_tile, None, CHUNK, D_MODEL), prompt_idx),
                   pl.BlockSpec((TM_FFN, D_MODEL), lambda i: (jnp.maximum(i - n_prompt_tiles, 0), 0))]
    joined_spec = pl.BlockSpec((TM_FFN, D_MODEL), lambda i: (i, 0))
    xs = tuple(x) if split_in else (x,)
    n_sample = xs[1].shape[0] if split_in else x.shape[0] - n_prompt_tiles * TM_FFN
    t = n_prompt_tiles * TM_FFN + n_sample
    split_shapes = [jax.ShapeDtypeStruct((batch, n_chunks, CHUNK, D_MODEL), F32),
                    jax.ShapeDtypeStruct((n_sample, D_MODEL), F32)]
    return pl.pallas_call(
        functools.partial(_ffn_body, final=final, n_prompt_tiles=n_prompt_tiles, split_in=split_in,
                          split_out=split_out),
        grid=(t // TM_FFN,),
        in_specs=(split_specs if split_in else [joined_spec]) + [
            pl.BlockSpec((None, 1, D_MODEL), lambda i: (layer, 0, 0)),
            pl.BlockSpec((None, D_MODEL, 2 * D_FF), lambda i: (layer, 0, 0), pipeline_mode=pl.Buffered(1)),
            pl.BlockSpec((None, D_FF, D_MODEL), lambda i: (layer, 0, 0), pipeline_mode=pl.Buffered(1)),
            pl.BlockSpec((1, D_MODEL), lambda i: (0, 0)),
        ],
        out_specs=split_specs if split_out else joined_spec,
        out_shape=split_shapes if split_out else jax.ShapeDtypeStruct((t, D_MODEL), F32),
        compiler_params=_cparams(("arbitrary",)),
        name="ffn",
    )(*xs, norm, w_up, w_down, final_norm)


def _inproj_body(x_ref, g_ref, w_ref, o_ref):
    h = _rms(x_ref[...], g_ref[...]).astype(BF16)
    o_ref[...] = _dot(h, w_ref[...])


def _inproj(x, norm, w_all, layer):
    t = x.shape[0]
    return pl.pallas_call(
        _inproj_body,
        grid=(P_COLS // PROJ_COL_TILE, t // TM_TOK),
        in_specs=[
            pl.BlockSpec((TM_TOK, D_MODEL), lambda j, i: (i, 0)),
            pl.BlockSpec((None, 1, D_MODEL), lambda j, i: (layer, 0, 0)),
            pl.BlockSpec((None, D_MODEL, PROJ_COL_TILE), lambda j, i: (layer, 0, j)),
        ],
        out_specs=pl.BlockSpec((TM_TOK, PROJ_COL_TILE), lambda j, i: (i, j)),
        out_shape=jax.ShapeDtypeStruct((t, P_COLS), F32),
        compiler_params=_cparams(("arbitrary", "arbitrary")),
        name="inproj",
    )(x, norm, w_all)


def _merge_body(x_ref, op_ref, os_ref, g0_ref, g1_ref, g2_ref, wb_ref, wo_ref, o_ref, *, n_prompt_tiles):
    is_prompt = pl.program_id(0) < n_prompt_tiles
    merged = None
    for c, g_ref in enumerate((g0_ref, g1_ref, g2_ref)):
        cs = slice(c * BRANCH_W, (c + 1) * BRANCH_W)
        br = jnp.where(is_prompt, op_ref[:, cs], os_ref[:, cs])
        term = jax.nn.sigmoid(g_ref[...]) * _dot(br, wb_ref[c])
        merged = term if merged is None else merged + term
    o_ref[...] = x_ref[...] + _dot(merged.astype(BF16), wo_ref[...])


def _merge(x, o_prompt, o_sample, p_all, w_branch, w_out, layer):
    t = x.shape[0]
    n_prompt_tiles = o_prompt.shape[0] // TM_TOK
    gate_blk = COL_GATE // D_MODEL

    def gate_spec(c):
        return pl.BlockSpec((TM_TOK, D_MODEL), lambda i: (i, gate_blk + c))

    return pl.pallas_call(
        functools.partial(_merge_body, n_prompt_tiles=n_prompt_tiles),
        grid=(t // TM_TOK,),
        in_specs=[
            pl.BlockSpec((TM_TOK, D_MODEL), lambda i: (i, 0)),
            pl.BlockSpec((TM_TOK, N_BRANCH * BRANCH_W), lambda i: (jnp.minimum(i, n_prompt_tiles - 1), 0)),
            pl.BlockSpec((TM_TOK, N_BRANCH * BRANCH_W), lambda i: (0, 0)),
            gate_spec(0), gate_spec(1), gate_spec(2),
            pl.BlockSpec((None, N_BRANCH, BRANCH_W, D_MODEL), lambda i: (layer, 0, 0, 0)),
            pl.BlockSpec((None, D_MODEL, D_MODEL), lambda i: (layer, 0, 0)),
        ],
        out_specs=pl.BlockSpec((TM_TOK, D_MODEL), lambda i: (i, 0)),
        out_shape=jax.ShapeDtypeStruct((t, D_MODEL), F32),
        compiler_params=_cparams(("arbitrary",)),
        name="merge",
    )(x, o_prompt, o_sample, p_all, p_all, p_all, w_branch, w_out)


def _gla_gate_body(sm_ref, wlr_ref, b_ref, o_ref):
    z = _dot(sm_ref[...].astype(BF16), wlr_ref[...]) + b_ref[...]
    o_ref[...] = _log_sigmoid(z) / GLA_GATE_NORM


def _gla_gate(p_all, wlr_pad, gla_b, layer):
    t = p_all.shape[0]
    n = HEADS * GLA_DK
    tm = TM_TOK * max(d for d in range(1, GATE_MAX_TILES + 1) if (t // TM_TOK) % d == 0)
    return pl.pallas_call(
        _gla_gate_body,
        grid=(t // tm,),
        in_specs=[
            pl.BlockSpec((tm, SMALL_W), lambda i: (i, COL_SMALL // SMALL_W)),
            pl.BlockSpec((None, SMALL_W, n), lambda i: (layer, 0, 0)),
            pl.BlockSpec((None, 1, n), lambda i: (layer, 0, 0)),
        ],
        out_specs=pl.BlockSpec((tm, n), lambda i: (i, 0)),
        out_shape=jax.ShapeDtypeStruct((t, n), F32),
        compiler_params=_cparams(("arbitrary",)),
        name="gla_gate",
    )(p_all, wlr_pad, gla_b)


def _layer_lower_bound(lb_raw, layer):
    e = jnp.exp(lb_raw - jnp.max(lb_raw, axis=0, keepdims=True))
    soft = e / jnp.sum(e, axis=0, keepdims=True)
    lb = jnp.zeros_like(soft[0:1])
    for j in range(1, layer + 1):
        lb = lb + soft[j:j + 1]
    return lb


def _head_norm(o, g):
    return o * lax.rsqrt(jnp.mean(o * o, axis=-1, keepdims=True) + EPS) * g


def _gated_linear_branch(q_of, k_of, v_of, logf, dk, dmat, masks, read_state, write_state, emit):
    level_masks, eye, _ = masks
    n_lev = len(level_masks)
    c = eye.shape[0]
    eye_dk = _eye(dk)
    e_all = _exact_dot(dmat, logf)
    for h in range(HEADS):
        ks = slice(h * dk, (h + 1) * dk)
        qh, kh, vh = q_of(h), k_of(h), v_of(h).astype(BF16)
        att = jnp.where(eye, _dot_nt(qh.astype(BF16), kh.astype(BF16)), 0.0)
        for lv in range(n_lev):
            a = jnp.exp(e_all[lv * c:(lv + 1) * c, ks])
            att = att + jnp.where(level_masks[lv], _dot_nt((qh * a).astype(BF16), (kh * a).astype(BF16)), 0.0)
        cum = e_all[n_lev * c:(n_lev + 1) * c, ks]
        rev = e_all[(n_lev + 1) * c:(n_lev + 2) * c, ks]
        st = read_state(h)
        emit(h, _dot(att.astype(BF16), vh) + _dot((qh * jnp.exp(cum)).astype(BF16), st.astype(BF16)))
        decay = _column_of(jnp.exp(cum[c - 1:c, :]), eye_dk)
        write_state(h, st * decay + _dot_tn((kh * jnp.exp(rev)).astype(BF16), vh))


def _mlstm_branch(q_of, k_of, v_of, gates, bcum, masks, read_state, write_state, emit):
    _, eye, tri = masks
    c = eye.shape[0]
    for h in range(HEADS):
        qh, kh, vh = q_of(h), k_of(h), v_of(h).astype(BF16)
        qb = qh.astype(BF16)
        bcol = bcum[:, SM_F + h:SM_F + h + 1]
        icol = gates[:, SM_I + h:SM_I + h + 1]
        log_d = jnp.where(tri, bcol + _row_of(icol - bcol, eye), NEG_BIG)
        cst, nrow, m_prev = read_state(h)
        inter = bcol + m_prev
        m_t = jnp.maximum(inter, jnp.max(log_d, axis=1, keepdims=True))
        d = jnp.exp(log_d - m_t)
        w_inter = jnp.exp(inter - m_t)
        qk = _dot_nt(qb, kh.astype(BF16)) * d
        num = _dot(qk.astype(BF16), vh) + w_inter * _dot(qb, cst.astype(BF16))
        den = jnp.sum(qk, axis=1, keepdims=True) + w_inter * jnp.sum(qh * nrow, axis=1, keepdims=True)
        emit(h, num / jnp.maximum(jnp.abs(den), jnp.exp(-m_t)))
        m_new = m_t[c - 1:c, :]
        b_end = bcol[c - 1:c, :]
        kw = jnp.exp(b_end - bcol + icol - m_new) * kh
        carry = jnp.exp(b_end + m_prev - m_new)
        write_state(h,
                    carry * cst + _dot_tn(kw.astype(BF16), vh),
                    carry * nrow + jnp.sum(kw, axis=0, keepdims=True),
                    m_new)


def _hgrn_log_decay(zf, lb):
    return jnp.log(lb + (1.0 - lb) * jax.nn.sigmoid(zf))


def _mlstm_gates(p_small, ml_bias):
    lane = lax.broadcasted_iota(jnp.int32, p_small.shape, p_small.ndim - 1)
    biased = p_small + ml_bias
    return jnp.where(lane >= SM_F, _log_sigmoid(biased), biased)


def _finish(branch, o, norm_g, gate_pre):
    return _head_norm(o, norm_g) * (jax.nn.sigmoid(gate_pre) if branch == 2 else _silu(gate_pre))


def _gate_pre(load, branch, h):
    name, c0 = (("hg", 3 * HEADS * HG_DK), ("gg", 0), ("ml", 3 * HEADS * ML_DK))[branch]
    return load[name](c0 + h * DV, c0 + (h + 1) * DV)


def _sequence_chunk(load, prm, dmat, c, state_io, emit):
    hg_io, gla_io, ml_io = state_io
    lb = prm["lb"]
    masks = _pair_masks(c)
    n_lev = len(masks[0])
    w, wg = HEADS * HG_DK, HEADS * GLA_DK
    head = lambda h, width=DV: (h * width, (h + 1) * width)
    hg, gqk, ml = load["hg"], load["gqk"], load["ml"]

    _gated_linear_branch(
        lambda h: _silu(hg(*head(h))),
        lambda h: (1.0 - lb[:, slice(*head(h))]) * jax.nn.sigmoid(-hg(w + h * HG_DK, w + (h + 1) * HG_DK)),
        lambda h: hg(2 * w + h * DV, 2 * w + (h + 1) * DV),
        _hgrn_log_decay(hg(w, 2 * w), lb),
        HG_DK, dmat, masks, hg_io[0], hg_io[1], functools.partial(emit, 0))

    _gated_linear_branch(
        lambda h: gqk(*head(h, GLA_DK)) * (GLA_DK ** -0.5),
        lambda h: gqk(wg + h * GLA_DK, wg + (h + 1) * GLA_DK),
        lambda h: load["gv"](*head(h)),
        load["ga"](0, wg),
        GLA_DK, dmat, masks, gla_io[0], gla_io[1], functools.partial(emit, 1))

    gates = _mlstm_gates(load["sm"](0, SMALL_W), prm["ml_bias"])
    _mlstm_branch(
        lambda h: ml(*head(h)),
        lambda h: ml(w + h * ML_DK, w + (h + 1) * ML_DK) * (ML_DK ** -0.5),
        lambda h: ml(2 * w + h * DV, 2 * w + (h + 1) * DV),
        gates, _exact_dot(dmat[n_lev * c:(n_lev + 1) * c, :], gates), masks, ml_io[0], ml_io[1],
        functools.partial(emit, 2))


def _state_io(b, in_refs, out_refs):
    ihg, igla, imc, imn, imm = in_refs
    ohg, ogla, omc, omn, omm = out_refs

    def write_to(ref):
        def write(h, s):
            ref[b, h] = s
        return write

    def ml_write(h, c_new, n_new, m_new):
        omc[b, h] = c_new
        omn[b, h] = n_new
        omm[b, h] = jnp.broadcast_to(m_new, (1, SMALL_W))

    return ((lambda h: ihg[b, h], write_to(ohg)),
            (lambda h: igla[b, h], write_to(ogla)),
            (lambda h: (imc[b, h], imn[b, h], imm[b, h][:, 0:1]), ml_write))


BLOCK_NAMES = ("hg", "ml", "gqk", "gv", "gg", "sm", "ga")
N_BLOCKS = len(BLOCK_NAMES)
PARAM_NAMES = ("hg_norm", "gla_norm", "ml_bias", "ml_norm")
N_PARAMS = 1 + len(PARAM_NAMES)
STATE_TAILS = [(HG_DK, DV), (GLA_DK, DV), (ML_DK, DV), (1, ML_DK), (1, SMALL_W)]
GL_SCRATCH = ("qs", "ks", "qe", "ke", "v", "dec")
ML_SCRATCH = ("qm", "km", "qw", "kw", "v", "emt", "carry")
SAFE_LOG_RANGE = 80.0
ML_SAFE_RANGE = 40.0


class _Tile(NamedTuple):
    n_seq: int
    c: int
    flat: bool
    unroll: int

    def shape(self, width):
        return (self.n_seq * self.c, width) if self.flat else (self.n_seq, self.c, width)

    def load(self, ref, c0, c1):
        if self.flat:
            return ref[:, c0:c1].reshape(self.n_seq, self.c, c1 - c0)
        return ref[:, :, c0:c1]

    def store(self, ref, c0, c1, x):
        if self.flat:
            ref[:, c0:c1] = x.reshape(self.n_seq * self.c, c1 - c0).astype(ref.dtype)
        else:
            ref[:, :, c0:c1] = x.astype(ref.dtype)

    def rows(self, b):
        return pl.ds(pl.multiple_of(b * self.c, self.c), self.c)

    def seq(self, ref, b, c0, c1):
        return ref[self.rows(b), c0:c1] if self.flat else ref[b, :, c0:c1]

    def seq_store(self, ref, b, c0, c1, x):
        if self.flat:
            ref[self.rows(b), c0:c1] = x
        else:
            ref[b, :, c0:c1] = x

    def scan(self, x, combine, fill):
        n_seq, c, w = x.shape
        if self.flat:
            y = x.reshape(n_seq * c, w)
            pos = lax.broadcasted_iota(jnp.int32, y.shape, 0) % c
            shift = 1
            while shift < c:
                y = combine(y, jnp.where(pos >= shift, pltpu.roll(y, shift, 0), fill))
                shift *= 2
            return y.reshape(n_seq, c, w)
        t = lax.broadcasted_iota(jnp.int32, x.shape, 1)
        acc = jnp.full(x.shape, fill, x.dtype)
        for j in range(c):
            acc = combine(acc, jnp.where(t >= j, x[:, j:j + 1, :], fill))
        return acc


def _max_all(x):
    for axis in (2, 1, 0):
        x = jnp.max(x, axis=axis, keepdims=True)
    return x


def _prepare_gated_linear(tile, q, k, v, logf, sc):
    c, n = tile.c, logf.shape[-1]
    cum = tile.scan(logf, jnp.add, 0.0)
    ref, end = cum[:, c // 2 - 1:c // 2, :], cum[:, c - 1:c, :]
    tile.store(sc["qs"], 0, n, q * jnp.exp(cum - ref))
    tile.store(sc["ks"], 0, n, k * jnp.exp(ref - cum))
    tile.store(sc["qe"], 0, n, q * jnp.exp(cum))
    tile.store(sc["ke"], 0, n, k * jnp.exp(end - cum))
    tile.store(sc["v"], 0, v.shape[-1], v)
    decay = jnp.transpose(jnp.exp(cum[:, c - 1, :]))
    for b in range(tile.n_seq):
        sc["dec"][b] = decay[:, b:b + 1]
    return _max_all(jnp.abs(cum - ref))


def _prepare_mlstm(tile, q, k, v, gates, m_state, sc):
    c = tile.c
    lane = lax.broadcasted_iota(jnp.int32, gates.shape, 2)
    g = jnp.where((lane >= SM_I) & (lane < SM_F + HEADS), gates, 0.0)
    bcum = tile.scan(g, jnp.add, 0.0)
    a = pltpu.roll(g, HEADS, 2) - bcum
    lane1 = lax.broadcasted_iota(jnp.int32, (tile.n_seq, 1, SMALL_W), 2)
    m_prev = jnp.zeros((tile.n_seq, 1, SMALL_W), F32)
    for h in range(HEADS):
        m_prev = jnp.where(lane1 == SM_F + h, m_state[:, h], m_prev)
    big_m = jnp.maximum(tile.scan(a, jnp.maximum, NEG_BIG), m_prev)
    ref, m_end = big_m[:, c // 2 - 1:c // 2, :], big_m[:, c - 1:c, :]
    scales = {"qm": jnp.exp(ref - big_m), "km": jnp.exp(a - ref),
              "qw": jnp.exp(m_prev - big_m), "kw": jnp.exp(a - m_end)}
    tile.store(sc["emt"], 0, SMALL_W, jnp.exp(-(bcum + big_m)))
    sc["carry"][...] = jnp.exp(m_prev - m_end)
    for h in range(HEADS):
        h0, h1 = h * ML_DK, (h + 1) * ML_DK
        qh, kh = q[:, :, h0:h1], k[:, :, h0:h1] * (ML_DK ** -0.5)
        for name, x in (("qm", qh), ("km", kh), ("qw", qh), ("kw", kh)):
            tile.store(sc[name], h0, h1, x * scales[name][:, :, SM_F + h:SM_F + h + 1])
    tile.store(sc["v"], 0, v.shape[-1], v)
    spread = jnp.where((lane >= SM_F) & (lane < SM_F + HEADS), jnp.abs(big_m - ref), 0.0)
    return _max_all(spread), bcum[:, c - 1:c, :] + m_end


def _gated_linear_units(tile, b, dk, sc, st_in, st_out, tri, emit):
    for h in range(HEADS):
        k0, k1 = h * dk, (h + 1) * dk
        att = jnp.where(tri, _dot_nt(tile.seq(sc["qs"], b, k0, k1), tile.seq(sc["ks"], b, k0, k1)), 0.0)
        vh = tile.seq(sc["v"], b, h * DV, (h + 1) * DV)
        st = st_in[b, h]
        emit(h, _dot(att.astype(BF16), vh) + _dot(tile.seq(sc["qe"], b, k0, k1), st.astype(BF16)))
        st_out[b, h] = st * sc["dec"][b, k0:k1, :] + _dot_tn(tile.seq(sc["ke"], b, k0, k1), vh)


def _mlstm_units(tile, b, sc, c_in, n_in, c_out, n_out, tri, emit):
    for h in range(HEADS):
        h0, h1 = h * ML_DK, (h + 1) * ML_DK
        att = jnp.where(tri, _dot_nt(tile.seq(sc["qm"], b, h0, h1), tile.seq(sc["km"], b, h0, h1)), 0.0)
        vh, qw, kw = (tile.seq(sc[name], b, h0, h1) for name in ("v", "qw", "kw"))
        cst, nrow = c_in[b, h], n_in[b, h]
        num = _dot(att.astype(BF16), vh) + _dot(qw, cst.astype(BF16))
        den = jnp.sum(att, axis=1, keepdims=True) + jnp.sum(qw.astype(F32) * nrow, axis=1, keepdims=True)
        emit(h, num / jnp.maximum(jnp.abs(den), tile.seq(sc["emt"], b, SM_F + h, SM_F + h + 1)))
        carry = sc["carry"][b][:, SM_F + h:SM_F + h + 1]
        c_out[b, h] = carry * cst + _dot_tn(kw, vh)
        n_out[b, h] = carry * nrow + jnp.sum(kw.astype(F32), axis=0, keepdims=True)


def _scan_body(*refs, layer, tile, zero_init, n_stacked=0):
    refs = list(refs)
    take_n = lambda n: [refs.pop(0) for _ in range(n)]
    blocks = dict(zip(BLOCK_NAMES, take_n(N_BLOCKS)))
    param_refs = take_n(N_PARAMS)
    dmat_ref, = take_n(1)
    in_states = None if zero_init else take_n(len(STATE_TAILS))
    take_n(n_stacked)
    o_ref, = take_n(1)
    out_states = take_n(len(STATE_TAILS))
    hg_sc = dict(zip(GL_SCRATCH, take_n(len(GL_SCRATCH))))
    gla_sc = dict(zip(GL_SCRATCH, take_n(len(GL_SCRATCH))))
    ml_sc = dict(zip(ML_SCRATCH, take_n(len(ML_SCRATCH))))
    raw_ref, = take_n(1)
    c, n_seq = tile.c, tile.n_seq

    if zero_init:
        in_states = out_states

        @pl.when(pl.program_id(0) == 0)
        def _():
            for r in out_states:
                r[...] = jnp.zeros_like(r)

    lb_ref, *rest = param_refs
    prm = dict(zip(PARAM_NAMES, (r[...] for r in rest)), lb=_layer_lower_bound(lb_ref[...], layer))
    lb = prm["lb"]
    col = {name: functools.partial(tile.load, ref) for name, ref in blocks.items()}
    w, wg = HEADS * HG_DK, HEADS * GLA_DK
    hg, gqk, ml = col["hg"], col["gqk"], col["ml"]

    zf = hg(w, 2 * w)
    gl_spread = jnp.maximum(
        _prepare_gated_linear(tile, _silu(hg(0, w)), (1.0 - lb) * jax.nn.sigmoid(-zf), hg(2 * w, 3 * w),
                              _hgrn_log_decay(zf, lb), hg_sc),
        _prepare_gated_linear(tile, gqk(0, wg) * (GLA_DK ** -0.5), gqk(wg, 2 * wg), col["gv"](0, HEADS * DV),
                              col["ga"](0, wg), gla_sc))
    ml_spread, m_new = _prepare_mlstm(tile, ml(0, w), ml(w, 2 * w), ml(2 * w, 3 * w),
                                      _mlstm_gates(col["sm"](0, SMALL_W), prm["ml_bias"]),
                                      in_states[4][...], ml_sc)
    worst = jnp.maximum(gl_spread * (1.0 / SAFE_LOG_RANGE), ml_spread * (1.0 / ML_SAFE_RANGE))
    tri = _pair_masks(c)[2]

    def emit_for(b):
        def emit(branch, h, val):
            c0 = branch * BRANCH_W + h * DV
            tile.seq_store(raw_ref, b, c0, c0 + DV, val)
        return emit

    def factored():
        def per_sequence(b, carry):
            emit = emit_for(b)
            _gated_linear_units(tile, b, HG_DK, hg_sc, in_states[0], out_states[0], tri, functools.partial(emit, 0))
            _gated_linear_units(tile, b, GLA_DK, gla_sc, in_states[1], out_states[1], tri, functools.partial(emit, 1))
            _mlstm_units(tile, b, ml_sc, in_states[2], in_states[3], out_states[2], out_states[3], tri,
                         functools.partial(emit, 2))
            return carry

        lax.fori_loop(0, n_seq, per_sequence, 0, unroll=tile.unroll)
        for h in range(HEADS):
            out_states[4][:, h] = jnp.broadcast_to(m_new[:, :, SM_F + h:SM_F + h + 1], (n_seq, 1, SMALL_W))

    def per_head():
        dmat = dmat_ref[...]

        def per_sequence(b, carry):
            load = {name: functools.partial(tile.seq, ref, b) for name, ref in blocks.items()}
            _sequence_chunk(load, prm, dmat, c, _state_io(b, in_states, out_states), emit_for(b))
            return carry

        lax.fori_loop(0, n_seq, per_sequence, 0)

    lax.cond(worst[0, 0, 0] < 1.0, factored, per_head)

    norms = (prm["hg_norm"], prm["gla_norm"], prm["ml_norm"])
    for branch in range(N_BRANCH):
        for h in range(HEADS):
            c0 = branch * BRANCH_W + h * DV
            tile.store(o_ref, c0, c0 + DV,
                       _finish(branch, tile.load(raw_ref, c0, c0 + DV), norms[branch], _gate_pre(col, branch, h)))


def _scan_scratch(tile):
    def gated_linear(dk):
        return ([pltpu.VMEM(tile.shape(HEADS * dk), BF16)] * 4
                + [pltpu.VMEM(tile.shape(HEADS * DV), BF16), pltpu.VMEM((tile.n_seq, HEADS * dk, 1), F32)])

    return (gated_linear(HG_DK) + gated_linear(GLA_DK) + [pltpu.VMEM(tile.shape(HEADS * ML_DK), BF16)] * 5
            + [pltpu.VMEM(tile.shape(SMALL_W), F32), pltpu.VMEM((tile.n_seq, 1, SMALL_W), F32),
               pltpu.VMEM(tile.shape(N_BRANCH * BRANCH_W), F32)])


def _param_specs(layer, depth):
    return [
        pl.BlockSpec((depth, HEADS * HG_DK), lambda *_: (0, 0)),
        pl.BlockSpec((None, 1, DV), lambda *_: (layer, 0, 0)),
        pl.BlockSpec((None, 1, DV), lambda *_: (layer, 0, 0)),
        pl.BlockSpec((None, 1, SMALL_W), lambda *_: (layer, 0, 0)),
        pl.BlockSpec((None, 1, DV), lambda *_: (layer, 0, 0)),
    ]


BLOCK_COLS = ((W_HG, COL_HG), (W_ML, COL_ML), (W_GLA_QK, COL_GLA_QK), (HEADS * DV, COL_GLA_V),
              (HEADS * DV, COL_GLA_G), (SMALL_W, COL_SMALL), (HEADS * GLA_DK, 0))


def _prompt_scan(p_all, log_a, params, dmat, layer, depth, batch, n_chunks):
    tile = _Tile(batch, CHUNK, True, batch)
    rows = batch * CHUNK
    full = lambda shape: pl.BlockSpec(shape, lambda c: (0,) * len(shape))
    state_shapes = [(batch, HEADS) + t for t in STATE_TAILS]
    return pl.pallas_call(
        functools.partial(_scan_body, layer=layer, tile=tile, zero_init=True),
        grid=(n_chunks,),
        in_specs=[pl.BlockSpec((rows, width), lambda c, blk=start // width: (c, blk)) for width, start in BLOCK_COLS]
        + _param_specs(layer, depth) + [full(dmat.shape)],
        out_specs=[pl.BlockSpec((rows, N_BRANCH * BRANCH_W), lambda c: (c, 0))] + [full(s) for s in state_shapes],
        out_shape=[jax.ShapeDtypeStruct((n_chunks * rows, N_BRANCH * BRANCH_W), BF16)]
        + [jax.ShapeDtypeStruct(s, F32) for s in state_shapes],
        scratch_shapes=_scan_scratch(tile),
        compiler_params=_cparams(("arbitrary",)),
        name="prompt_scan",
    )(*([p_all] * (N_BLOCKS - 1)), log_a, *params, dmat)


N_STACKED = 3


def _sample_scan(p_s, log_a, params, dmat, states, stacks, layer, depth):
    n_seq, seq, _ = p_s.shape
    nb = SAMPLE_NB
    tile = _Tile(nb, seq, False, 2)
    full = lambda shape: pl.BlockSpec(shape, lambda i: (0,) * len(shape))
    layer_block = lambda t: pl.BlockSpec((None, nb, HEADS) + t, lambda i: (layer, i, 0, 0, 0))
    first_stack = N_BLOCKS + N_PARAMS + 1 + len(STATE_TAILS)
    return pl.pallas_call(
        functools.partial(_scan_body, layer=layer, tile=tile, zero_init=False, n_stacked=N_STACKED),
        grid=(n_seq // nb,),
        in_specs=[pl.BlockSpec((nb, seq, width), lambda i, blk=start // width: (i, 0, blk)) for width, start in BLOCK_COLS]
        + _param_specs(layer, depth) + [full(dmat.shape)] + [layer_block(t) for t in STATE_TAILS]
        + [pl.BlockSpec(memory_space=pl.ANY)] * N_STACKED,
        out_specs=[pl.BlockSpec((nb, seq, N_BRANCH * BRANCH_W), lambda i: (i, 0, 0))]
        + [layer_block(t) for t in STATE_TAILS[:N_STACKED]]
        + [pl.BlockSpec((nb, HEADS) + t, lambda i: (i, 0, 0, 0)) for t in STATE_TAILS[N_STACKED:]],
        out_shape=[jax.ShapeDtypeStruct((n_seq, seq, N_BRANCH * BRANCH_W), BF16)]
        + [jax.ShapeDtypeStruct(s.shape, F32) for s in stacks]
        + [jax.ShapeDtypeStruct((n_seq, HEADS) + t, F32) for t in STATE_TAILS[N_STACKED:]],
        input_output_aliases={first_stack + k: 1 + k for k in range(N_STACKED)},
        scratch_shapes=_scan_scratch(tile),
        compiler_params=_cparams(("arbitrary",)),
        name="sample_scan",
    )(*([p_s] * (N_BLOCKS - 1)), log_a, *params, dmat, *states, *stacks)


SRC_GLA = W_HG
SRC_LR = SRC_GLA + W_GLA
SRC_ML = SRC_LR + GLA_RANK
SRC_IF = SRC_ML + W_ML
SRC_GATE = SRC_IF + 2 * HEADS
W_IN_MOVES = ((COL_HG, 0, W_HG), (COL_ML, SRC_ML, W_ML), (COL_GATE, SRC_GATE, W_GATE), (COL_GLA_QK, SRC_GLA, W_GLA))
W_IN_SMALL = ((SRC_LR, GLA_RANK), (SRC_IF, 2 * HEADS))
REGROUP_ROWS = 256
REGROUP_PIECE = 512


def _regroup_body(wt_ref, o_ref):
    rows = o_ref.shape[0]
    eye = _eye(rows).astype(BF16)
    transposed = lambda piece: _dot_nt(eye, piece.astype(BF16)).astype(BF16)
    for dst, src, n in W_IN_MOVES:
        for c in range(0, n, REGROUP_PIECE):
            m = min(REGROUP_PIECE, n - c)
            o_ref[:, dst + c:dst + c + m] = transposed(wt_ref[src + c:src + c + m, :])
    pieces = [wt_ref[src:src + n, :] for src, n in W_IN_SMALL]
    pad = SMALL_W - sum(n for _, n in W_IN_SMALL)
    o_ref[:, COL_SMALL:] = transposed(jnp.concatenate(pieces + [jnp.zeros((pad, rows), F32)], axis=0))


def _regroup_w_in(w_in):
    depth, d, d_in = w_in.shape
    assert d_in == sum(n for _, _, n in W_IN_MOVES) + sum(n for _, n in W_IN_SMALL) and d % REGROUP_ROWS == 0
    return pl.pallas_call(
        _regroup_body,
        grid=(depth, d // REGROUP_ROWS),
        in_specs=[pl.BlockSpec((None, d_in, REGROUP_ROWS), lambda l, i: (l, 0, i))],
        out_specs=pl.BlockSpec((None, REGROUP_ROWS, P_COLS), lambda l, i: (l, i, 0)),
        out_shape=jax.ShapeDtypeStruct((depth, d, P_COLS), BF16),
        compiler_params=_cparams(("arbitrary", "arbitrary")),
        name="regroup_w_in",
    )(jnp.swapaxes(w_in, 1, 2))


def kernel(x_prompt, x_sample, state_hgrn, state_gla, state_mlstm_C, state_mlstm_n, state_mlstm_m,
           ffn1_norm, ffn1_w_up, ffn1_w_down, mix_norm, w_in, hgrn_lb_raw, hgrn_out_norm,
           gla_w_gate_lr, gla_b_gate, gla_out_norm, mlstm_b_i, mlstm_b_f, mlstm_out_norm,
           w_branch, w_out, ffn2_norm, ffn2_w_up, ffn2_w_down, final_norm):
    depth = w_in.shape[0]
    batch, seq, _ = x_prompt.shape
    n_seq, dec_seq, _ = x_sample.shape
    assert seq % CHUNK == 0 and dec_seq % CHUNK != 0 and dec_seq & (dec_seq - 1) == 0
    assert (batch * CHUNK) % TM_TOK == 0 and n_seq * dec_seq == TM_TOK and n_seq % SAMPLE_NB == 0
    n_chunks = seq // CHUNK
    n_prompt = batch * seq

    assert TM_FFN % CHUNK == 0 and batch % (TM_FFN // CHUNK) == 0 and (n_seq * dec_seq) % TM_FFN == 0
    x = (x_prompt.reshape(batch, n_chunks, CHUNK, D_MODEL), x_sample.reshape(n_seq * dec_seq, D_MODEL))

    row3 = lambda a: a.reshape(a.shape[0], 1, a.shape[-1])
    w_all = _regroup_w_in(w_in)
    wlr_pad = jnp.pad(gla_w_gate_lr, ((0, 0), (0, SMALL_W - GLA_RANK), (0, 0))).astype(BF16)
    ml_bias = jnp.pad(jnp.concatenate([mlstm_b_i, mlstm_b_f], axis=-1),
                      ((0, 0), (SM_I, SMALL_W - SM_I - 2 * HEADS)))
    scan_params = (hgrn_lb_raw, row3(hgrn_out_norm), row3(gla_out_norm), row3(ml_bias), row3(mlstm_out_norm))
    gla_b3 = row3(gla_b_gate)
    dmat_p = jnp.asarray(_decay_matrix(CHUNK), BF16)
    dmat_s = jnp.asarray(_decay_matrix(dec_seq), BF16)
    sample_states = (state_hgrn, state_gla, state_mlstm_C,
                     state_mlstm_n.reshape(depth, n_seq, HEADS, 1, ML_DK),
                     jnp.broadcast_to(state_mlstm_m[..., None, None], (depth, n_seq, HEADS, 1, SMALL_W)))
    ffn_w = [(row3(ffn1_norm), ffn1_w_up.astype(BF16), ffn1_w_down.astype(BF16)),
             (row3(ffn2_norm), ffn2_w_up.astype(BF16), ffn2_w_down.astype(BF16))]
    w_branch_b, w_out_b, mix_norm3 = w_branch.astype(BF16), w_out.astype(BF16), row3(mix_norm)
    fin = final_norm.reshape(1, D_MODEL)

    p_states, s_small = [], []
    s_stacks = [jnp.zeros((depth, n_seq, HEADS) + t, F32) for t in STATE_TAILS[:N_STACKED]]
    for l in range(depth):
        x = _ffn(x, *ffn_w[0], fin, l, False, (batch, n_chunks), split_in=l == 0)
        p_all = _inproj(x, mix_norm3, w_all, l)
        log_a = _gla_gate(p_all, wlr_pad, gla_b3, l)
        o_p, *ps = _prompt_scan(p_all, log_a, scan_params, dmat_p, l, depth, batch, n_chunks)
        p_s = p_all[n_prompt:].reshape(n_seq, dec_seq, P_COLS)
        log_a_s = log_a[n_prompt:].reshape(n_seq, dec_seq, HEADS * GLA_DK)
        o_s, *ss = _sample_scan(p_s, log_a_s, scan_params, dmat_s, sample_states, s_stacks, l, depth)
        s_stacks = ss[:N_STACKED]
        x = _merge(x, o_p, o_s.reshape(n_seq * dec_seq, N_BRANCH * BRANCH_W), p_all, w_branch_b, w_out_b, l)
        last = l == depth - 1
        x = _ffn(x, *ffn_w[1], fin, l, last, (batch, n_chunks), split_out=last)
        p_states.append(ps)
        s_small.append(ss[N_STACKED:])

    y_prompt = x[0].reshape(batch, seq, D_MODEL)
    y_sample = x[1].reshape(n_seq, dec_seq, D_MODEL)

    stack = lambda states, i: jnp.stack([st[i] for st in states])
    vectors = lambda mn, mm: (mn[..., 0, :], mm[..., 0, 0])
    prompt_out = tuple(stack(p_states, i) for i in range(N_STACKED)) + vectors(stack(p_states, 3), stack(p_states, 4))
    sample_out = tuple(s_stacks) + vectors(stack(s_small, 0), stack(s_small, 1))
    return (y_prompt, y_sample) + prompt_out + sample_out
```

```python
import functools
from typing import NamedTuple

import numpy as np
import jax
import jax.numpy as jnp
from jax import lax
from jax.experimental import pallas as pl
from jax.experimental.pallas import tpu as pltpu

D_MODEL = 1024
HEADS = 4
HG_DK = 128
GLA_DK = 64
GLA_RANK = 16
GLA_GATE_NORM = 16.0
ML_DK = 128
DV = 128
BRANCH_W = 512
N_BRANCH = 3
D_FF = 2816
CHUNK = 64
EPS = 1e-6
NEG_BIG = -1e30

F32 = jnp.float32
BF16 = jnp.bfloat16

W_HG = 4 * HEADS * HG_DK
W_ML = 4 * HEADS * ML_DK
W_GATE = N_BRANCH * D_MODEL
W_GLA_QK = 2 * HEADS * GLA_DK
W_GLA = W_GLA_QK + 2 * HEADS * DV
SMALL_W = 128
COL_HG = 0
COL_ML = COL_HG + W_HG
COL_GATE = COL_ML + W_ML
COL_GLA_QK = COL_GATE + W_GATE
COL_GLA_V = COL_GLA_QK + W_GLA_QK
COL_GLA_G = COL_GLA_V + HEADS * DV
COL_SMALL = COL_GLA_QK + W_GLA
P_COLS = COL_SMALL + SMALL_W
SM_I = GLA_RANK
SM_F = GLA_RANK + HEADS

TM_FFN = 512
TM_TOK = 512
TM_MERGE = 256
PROJ_COL_TILE = P_COLS // 3
GATE_MAX_TILES = 11
SAMPLE_NB = 8
VMEM_LIMIT = 56 * 1024 * 1024


def _cparams(sem):
    return pltpu.CompilerParams(dimension_semantics=sem, vmem_limit_bytes=VMEM_LIMIT)


def _dot(a, b):
    return jnp.dot(a, b, preferred_element_type=F32)


def _dot_nt(a, b):
    return lax.dot_general(a, b, (((1,), (1,)), ((), ())), preferred_element_type=F32)


def _dot_tn(a, b):
    return lax.dot_general(a, b, (((0,), (0,)), ((), ())), preferred_element_type=F32)


def _rms(x, g):
    return x * lax.rsqrt(jnp.mean(x * x, axis=-1, keepdims=True) + EPS) * g


def _log_sigmoid(x):
    return jnp.minimum(x, 0.0) - jnp.log1p(jnp.exp(-jnp.abs(x)))


def _silu(x):
    return x * jax.nn.sigmoid(x)


def _exact_dot(m_bf16, x):
    hi = x.astype(BF16)
    r1 = x - hi.astype(F32)
    mid = r1.astype(BF16)
    lo = (r1 - mid.astype(F32)).astype(BF16)
    return _dot(m_bf16, hi) + _dot(m_bf16, mid) + _dot(m_bf16, lo)


def _level_sizes(c):
    out, m = [], c // 2
    while m >= 1:
        out.append(m)
        m //= 2
    return out


def _decay_matrix(c):
    blocks = []
    for m in _level_sizes(c):
        mat = np.zeros((c, c), np.float32)
        for t in range(c):
            mid = (t // (2 * m)) * (2 * m) + m
            if t >= mid:
                mat[t, mid:t + 1] = 1.0
            else:
                mat[t, t + 1:mid] = 1.0
        blocks.append(mat)
    blocks.append(np.tril(np.ones((c, c), np.float32)))
    blocks.append(np.triu(np.ones((c, c), np.float32), 1))
    return np.concatenate(blocks, axis=0)


def _eye(n):
    return lax.broadcasted_iota(jnp.int32, (n, n), 0) == lax.broadcasted_iota(jnp.int32, (n, n), 1)


def _pair_masks(c):
    ti = lax.broadcasted_iota(jnp.int32, (c, c), 0)
    si = lax.broadcasted_iota(jnp.int32, (c, c), 1)
    levels = []
    for m in _level_sizes(c):
        same = (ti // (2 * m)) == (si // (2 * m))
        levels.append(same & ((ti & m) != 0) & ((si & m) == 0))
    return levels, ti == si, si <= ti


def _column_of(row, eye):
    return jnp.sum(jnp.where(eye, row, 0.0), axis=1, keepdims=True)


def _row_of(col, eye):
    return jnp.sum(jnp.where(eye, col, 0.0), axis=0, keepdims=True)


def _ffn_body(*refs, final, n_prompt_tiles, split_in, split_out):
    refs = list(refs)
    x_refs = [refs.pop(0) for _ in range(2 if split_in else 1)]
    g_ref, wup_ref, wdn_ref, fin_ref, *o_refs = refs
    is_prompt = pl.program_id(0) < n_prompt_tiles
    if split_in:
        x = jnp.where(is_prompt, x_refs[0][...].reshape(TM_FFN, D_MODEL), x_refs[1][...])
    else:
        x = x_refs[0][...]
    h = _rms(x, g_ref[...]).astype(BF16)
    gu = _dot(h, wup_ref[...])
    act = _silu(gu[:, :D_FF]) * gu[:, D_FF:]
    out = x + 0.5 * _dot(act.astype(BF16), wdn_ref[...])
    if final:
        out = _rms(out, fin_ref[...])
    if split_out:
        @pl.when(is_prompt)
        def _():
            o_refs[0][...] = out.reshape(o_refs[0].shape)

        @pl.when(jnp.logical_not(is_prompt))
        def _():
            o_refs[1][...] = out
    else:
        o_refs[0][...] = out


def _ffn(x, norm, w_up, w_down, final_norm, layer, final, prompt_shape, split_in=False, split_out=False):
    batch, n_chunks = prompt_shape
    n_prompt_tiles = batch * n_chunks * CHUNK // TM_FFN
    seqs_per_tile = TM_FFN // CHUNK
    tiles_per_chunk = batch // seqs_per_tile

    def prompt_idx(i):
        j = jnp.minimum(i, n_prompt_tiles - 1)
        return (j % tiles_per_chunk, j // tiles_per_chunk, 0, 0)

    split_specs = [pl.BlockSpec((seqs_per_tile, None, CHUNK, D_MODEL), prompt_idx),
                   pl.BlockSpec((TM_FFN, D_MODEL), lambda i: (jnp.maximum(i - n_prompt_tiles, 0), 0))]
    joined_spec = pl.BlockSpec((TM_FFN, D_MODEL), lambda i: (i, 0))
    xs = tuple(x) if split_in else (x,)
    n_sample = xs[1].shape[0] if split_in else x.shape[0] - n_prompt_tiles * TM_FFN
    t = n_prompt_tiles * TM_FFN + n_sample
    split_shapes = [jax.ShapeDtypeStruct((batch, n_chunks, CHUNK, D_MODEL), F32),
                    jax.ShapeDtypeStruct((n_sample, D_MODEL), F32)]
    return pl.pallas_call(
        functools.partial(_ffn_body, final=final, n_prompt_tiles=n_prompt_tiles, split_in=split_in,
                          split_out=split_out),
        grid=(t // TM_FFN,),
        in_specs=(split_specs if split_in else [joined_spec]) + [
            pl.BlockSpec((None, 1, D_MODEL), lambda i: (layer, 0, 0)),
            pl.BlockSpec((None, D_MODEL, 2 * D_FF), lambda i: (layer, 0, 0), pipeline_mode=pl.Buffered(1)),
            pl.BlockSpec((None, D_FF, D_MODEL), lambda i: (layer, 0, 0), pipeline_mode=pl.Buffered(1)),
            pl.BlockSpec((1, D_MODEL), lambda i: (0, 0)),
        ],
        out_specs=split_specs if split_out else joined_spec,
        out_shape=split_shapes if split_out else jax.ShapeDtypeStruct((t, D_MODEL), F32),
        compiler_params=_cparams(("arbitrary",)),
        name="ffn",
    )(*xs, norm, w_up, w_down, final_norm)


def _inproj_body(x_ref, g_ref, w_ref, o_ref):
    h = _rms(x_ref[...], g_ref[...]).astype(BF16)
    o_ref[...] = _dot(h, w_ref[...])


def _inproj(x, norm, w_all, layer):
    t = x.shape[0]
    return pl.pallas_call(
        _inproj_body,
        grid=(P_COLS // PROJ_COL_TILE, t // TM_TOK),
        in_specs=[
            pl.BlockSpec((TM_TOK, D_MODEL), lambda j, i: (i, 0)),
            pl.BlockSpec((None, 1, D_MODEL), lambda j, i: (layer, 0, 0)),
            pl.BlockSpec((None, D_MODEL, PROJ_COL_TILE), lambda j, i: (layer, 0, j)),
        ],
        out_specs=pl.BlockSpec((TM_TOK, PROJ_COL_TILE), lambda j, i: (i, j)),
        out_shape=jax.ShapeDtypeStruct((t, P_COLS), F32),
        compiler_params=_cparams(("arbitrary", "arbitrary")),
        name="inproj",
    )(x, norm, w_all)


def _merge_body(x_ref, op_ref, os_ref, g0_ref, g1_ref, g2_ref, wb_ref, wo_ref, o_ref, *, n_prompt_tiles):
    is_prompt = pl.program_id(0) < n_prompt_tiles
    merged = None
    for c, g_ref in enumerate((g0_ref, g1_ref, g2_ref)):
        cs = slice(c * BRANCH_W, (c + 1) * BRANCH_W)
        br = jnp.where(is_prompt, op_ref[:, cs], os_ref[:, cs])
        term = jax.nn.sigmoid(g_ref[...]) * _dot(br, wb_ref[c])
        merged = term if merged is None else merged + term
    o_ref[...] = x_ref[...] + _dot(merged.astype(BF16), wo_ref[...])


def _merge(x, o_prompt, o_sample, p_all, w_branch, w_out, layer):
    t = x.shape[0]
    n_prompt_tiles = o_prompt.shape[0] // TM_MERGE
    gate_blk = COL_GATE // D_MODEL

    def gate_spec(c):
        return pl.BlockSpec((TM_MERGE, D_MODEL), lambda i: (i, gate_blk + c))

    return pl.pallas_call(
        functools.partial(_merge_body, n_prompt_tiles=n_prompt_tiles),
        grid=(t // TM_MERGE,),
        in_specs=[
            pl.BlockSpec((TM_MERGE, D_MODEL), lambda i: (i, 0)),
            pl.BlockSpec((TM_MERGE, N_BRANCH * BRANCH_W), lambda i: (jnp.minimum(i, n_prompt_tiles - 1), 0)),
            pl.BlockSpec((TM_MERGE, N_BRANCH * BRANCH_W), lambda i: (jnp.maximum(i - n_prompt_tiles, 0), 0)),
            gate_spec(0), gate_spec(1), gate_spec(2),
            pl.BlockSpec((None, N_BRANCH, BRANCH_W, D_MODEL), lambda i: (layer, 0, 0, 0)),
            pl.BlockSpec((None, D_MODEL, D_MODEL), lambda i: (layer, 0, 0)),
        ],
        out_specs=pl.BlockSpec((TM_MERGE, D_MODEL), lambda i: (i, 0)),
        out_shape=jax.ShapeDtypeStruct((t, D_MODEL), F32),
        compiler_params=_cparams(("arbitrary",)),
        name="merge",
    )(x, o_prompt, o_sample, p_all, p_all, p_all, w_branch, w_out)


def _gla_gate_body(sm_ref, wlr_ref, b_ref, o_ref):
    z = _dot(sm_ref[...].astype(BF16), wlr_ref[...]) + b_ref[...]
    o_ref[...] = _log_sigmoid(z) / GLA_GATE_NORM


def _gla_gate(p_all, wlr_pad, gla_b, layer):
    t = p_all.shape[0]
    n = HEADS * GLA_DK
    tm = TM_TOK * max(d for d in range(1, GATE_MAX_TILES + 1) if (t // TM_TOK) % d == 0)
    return pl.pallas_call(
        _gla_gate_body,
        grid=(t // tm,),
        in_specs=[
            pl.BlockSpec((tm, SMALL_W), lambda i: (i, COL_SMALL // SMALL_W)),
            pl.BlockSpec((None, SMALL_W, n), lambda i: (layer, 0, 0)),
            pl.BlockSpec((None, 1, n), lambda i: (layer, 0, 0)),
        ],
        out_specs=pl.BlockSpec((tm, n), lambda i: (i, 0)),
        out_shape=jax.ShapeDtypeStruct((t, n), F32),
        compiler_params=_cparams(("arbitrary",)),
        name="gla_gate",
    )(p_all, wlr_pad, gla_b)


def _layer_lower_bound(lb_raw, layer):
    e = jnp.exp(lb_raw - jnp.max(lb_raw, axis=0, keepdims=True))
    soft = e / jnp.sum(e, axis=0, keepdims=True)
    lb = jnp.zeros_like(soft[0:1])
    for j in range(1, layer + 1):
        lb = lb + soft[j:j + 1]
    return lb


def _head_norm(o, g):
    return o * lax.rsqrt(jnp.mean(o * o, axis=-1, keepdims=True) + EPS) * g


def _gated_linear_branch(q_of, k_of, v_of, logf, dk, dmat, masks, read_state, write_state, emit):
    level_masks, eye, _ = masks
    n_lev = len(level_masks)
    c = eye.shape[0]
    eye_dk = _eye(dk)
    e_all = _exact_dot(dmat, logf)
    for h in range(HEADS):
        ks = slice(h * dk, (h + 1) * dk)
        qh, kh, vh = q_of(h), k_of(h), v_of(h).astype(BF16)
        att = jnp.where(eye, _dot_nt(qh.astype(BF16), kh.astype(BF16)), 0.0)
        for lv in range(n_lev):
            a = jnp.exp(e_all[lv * c:(lv + 1) * c, ks])
            att = att + jnp.where(level_masks[lv], _dot_nt((qh * a).astype(BF16), (kh * a).astype(BF16)), 0.0)
        cum = e_all[n_lev * c:(n_lev + 1) * c, ks]
        rev = e_all[(n_lev + 1) * c:(n_lev + 2) * c, ks]
        st = read_state(h)
        emit(h, _dot(att.astype(BF16), vh) + _dot((qh * jnp.exp(cum)).astype(BF16), st.astype(BF16)))
        decay = _column_of(jnp.exp(cum[c - 1:c, :]), eye_dk)
        write_state(h, st * decay + _dot_tn((kh * jnp.exp(rev)).astype(BF16), vh))


def _mlstm_branch(q_of, k_of, v_of, gates, bcum, masks, read_state, write_state, emit):
    _, eye, tri = masks
    c = eye.shape[0]
    for h in range(HEADS):
        qh, kh, vh = q_of(h), k_of(h), v_of(h).astype(BF16)
        qb = qh.astype(BF16)
        bcol = bcum[:, SM_F + h:SM_F + h + 1]
        icol = gates[:, SM_I + h:SM_I + h + 1]
        log_d = jnp.where(tri, bcol + _row_of(icol - bcol, eye), NEG_BIG)
        cst, nrow, m_prev = read_state(h)
        inter = bcol + m_prev
        m_t = jnp.maximum(inter, jnp.max(log_d, axis=1, keepdims=True))
        d = jnp.exp(log_d - m_t)
        w_inter = jnp.exp(inter - m_t)
        qk = _dot_nt(qb, kh.astype(BF16)) * d
        num = _dot(qk.astype(BF16), vh) + w_inter * _dot(qb, cst.astype(BF16))
        den = jnp.sum(qk, axis=1, keepdims=True) + w_inter * jnp.sum(qh * nrow, axis=1, keepdims=True)
        emit(h, num / jnp.maximum(jnp.abs(den), jnp.exp(-m_t)))
        m_new = m_t[c - 1:c, :]
        b_end = bcol[c - 1:c, :]
        kw = jnp.exp(b_end - bcol + icol - m_new) * kh
        carry = jnp.exp(b_end + m_prev - m_new)
        write_state(h,
                    carry * cst + _dot_tn(kw.astype(BF16), vh),
                    carry * nrow + jnp.sum(kw, axis=0, keepdims=True),
                    m_new)


def _hgrn_log_decay(zf, lb):
    return jnp.log(lb + (1.0 - lb) * jax.nn.sigmoid(zf))


def _mlstm_gates(p_small, ml_bias):
    lane = lax.broadcasted_iota(jnp.int32, p_small.shape, p_small.ndim - 1)
    biased = p_small + ml_bias
    return jnp.where(lane >= SM_F, _log_sigmoid(biased), biased)


def _finish(branch, o, norm_g, gate_pre):
    return _head_norm(o, norm_g) * (jax.nn.sigmoid(gate_pre) if branch == 2 else _silu(gate_pre))


def _gate_pre(load, branch, h):
    name, c0 = (("hg", 3 * HEADS * HG_DK), ("gg", 0), ("ml", 3 * HEADS * ML_DK))[branch]
    return load[name](c0 + h * DV, c0 + (h + 1) * DV)


def _sequence_chunk(load, prm, dmat, c, state_io, emit):
    hg_io, gla_io, ml_io = state_io
    lb = prm["lb"]
    masks = _pair_masks(c)
    n_lev = len(masks[0])
    w, wg = HEADS * HG_DK, HEADS * GLA_DK
    head = lambda h, width=DV: (h * width, (h + 1) * width)
    hg, gqk, ml = load["hg"], load["gqk"], load["ml"]

    _gated_linear_branch(
        lambda h: _silu(hg(*head(h))),
        lambda h: (1.0 - lb[:, slice(*head(h))]) * jax.nn.sigmoid(-hg(w + h * HG_DK, w + (h + 1) * HG_DK)),
        lambda h: hg(2 * w + h * DV, 2 * w + (h + 1) * DV),
        _hgrn_log_decay(hg(w, 2 * w), lb),
        HG_DK, dmat, masks, hg_io[0], hg_io[1], functools.partial(emit, 0))

    _gated_linear_branch(
        lambda h: gqk(*head(h, GLA_DK)) * (GLA_DK ** -0.5),
        lambda h: gqk(wg + h * GLA_DK, wg + (h + 1) * GLA_DK),
        lambda h: load["gv"](*head(h)),
        load["ga"](0, wg),
        GLA_DK, dmat, masks, gla_io[0], gla_io[1], functools.partial(emit, 1))

    gates = _mlstm_gates(load["sm"](0, SMALL_W), prm["ml_bias"])
    _mlstm_branch(
        lambda h: ml(*head(h)),
        lambda h: ml(w + h * ML_DK, w + (h + 1) * ML_DK) * (ML_DK ** -0.5),
        lambda h: ml(2 * w + h * DV, 2 * w + (h + 1) * DV),
        gates, _exact_dot(dmat[n_lev * c:(n_lev + 1) * c, :], gates), masks, ml_io[0], ml_io[1],
        functools.partial(emit, 2))


def _state_io(b, in_refs, out_refs):
    ihg, igla, imc, imn, imm = in_refs
    ohg, ogla, omc, omn, omm = out_refs

    def write_to(ref):
        def write(h, s):
            ref[b, h] = s
        return write

    def ml_write(h, c_new, n_new, m_new):
        omc[b, h] = c_new
        omn[b, h] = n_new
        omm[b, h] = jnp.broadcast_to(m_new, (1, SMALL_W))

    return ((lambda h: ihg[b, h], write_to(ohg)),
            (lambda h: igla[b, h], write_to(ogla)),
            (lambda h: (imc[b, h], imn[b, h], imm[b, h][:, 0:1]), ml_write))


BLOCK_NAMES = ("hg", "ml", "gqk", "gv", "gg", "sm", "ga")
N_BLOCKS = len(BLOCK_NAMES)
PARAM_NAMES = ("hg_norm", "gla_norm", "ml_bias", "ml_norm")
N_PARAMS = 1 + len(PARAM_NAMES)
STATE_TAILS = [(HG_DK, DV), (GLA_DK, DV), (ML_DK, DV), (1, ML_DK), (1, SMALL_W)]
GL_SCRATCH = ("qs", "ks", "qe", "ke", "v", "dec")
ML_SCRATCH = ("qm", "km", "qw", "kw", "v", "emt", "carry")
SAFE_LOG_RANGE = 80.0
ML_SAFE_RANGE = 40.0


class _Tile(NamedTuple):
    n_seq: int
    c: int
    flat: bool
    unroll: int

    def shape(self, width):
        return (self.n_seq * self.c, width) if self.flat else (self.n_seq, self.c, width)

    def load(self, ref, c0, c1):
        if self.flat:
            return ref[:, c0:c1].reshape(self.n_seq, self.c, c1 - c0)
        return ref[:, :, c0:c1]

    def store(self, ref, c0, c1, x):
        if self.flat:
            ref[:, c0:c1] = x.reshape(self.n_seq * self.c, c1 - c0).astype(ref.dtype)
        else:
            ref[:, :, c0:c1] = x.astype(ref.dtype)

    def rows(self, b):
        return pl.ds(pl.multiple_of(b * self.c, self.c), self.c)

    def seq(self, ref, b, c0, c1):
        return ref[self.rows(b), c0:c1] if self.flat else ref[b, :, c0:c1]

    def seq_store(self, ref, b, c0, c1, x):
        if self.flat:
            ref[self.rows(b), c0:c1] = x
        else:
            ref[b, :, c0:c1] = x

    def scan(self, x, combine, fill):
        n_seq, c, w = x.shape
        if self.flat:
            y = x.reshape(n_seq * c, w)
            pos = lax.broadcasted_iota(jnp.int32, y.shape, 0) % c
            shift = 1
            while shift < c:
                y = combine(y, jnp.where(pos >= shift, pltpu.roll(y, shift, 0), fill))
                shift *= 2
            return y.reshape(n_seq, c, w)
        t = lax.broadcasted_iota(jnp.int32, x.shape, 1)
        acc = jnp.full(x.shape, fill, x.dtype)
        for j in range(c):
            acc = combine(acc, jnp.where(t >= j, x[:, j:j + 1, :], fill))
        return acc


def _max_all(x):
    for axis in (2, 1, 0):
        x = jnp.max(x, axis=axis, keepdims=True)
    return x


def _prepare_gated_linear(tile, q, k, v, logf, sc):
    c, n = tile.c, logf.shape[-1]
    cum = tile.scan(logf, jnp.add, 0.0)
    ref, end = cum[:, c // 2 - 1:c // 2, :], cum[:, c - 1:c, :]
    tile.store(sc["qs"], 0, n, q * jnp.exp(cum - ref))
    tile.store(sc["ks"], 0, n, k * jnp.exp(ref - cum))
    tile.store(sc["qe"], 0, n, q * jnp.exp(cum))
    tile.store(sc["ke"], 0, n, k * jnp.exp(end - cum))
    tile.store(sc["v"], 0, v.shape[-1], v)
    decay = jnp.transpose(jnp.exp(cum[:, c - 1, :]))
    for b in range(tile.n_seq):
        sc["dec"][b] = decay[:, b:b + 1]
    return _max_all(jnp.abs(cum - ref))


def _prepare_mlstm(tile, q, k, v, gates, m_state, sc):
    c = tile.c
    lane = lax.broadcasted_iota(jnp.int32, gates.shape, 2)
    g = jnp.where((lane >= SM_I) & (lane < SM_F + HEADS), gates, 0.0)
    bcum = tile.scan(g, jnp.add, 0.0)
    a = pltpu.roll(g, HEADS, 2) - bcum
    lane1 = lax.broadcasted_iota(jnp.int32, (tile.n_seq, 1, SMALL_W), 2)
    m_prev = jnp.zeros((tile.n_seq, 1, SMALL_W), F32)
    for h in range(HEADS):
        m_prev = jnp.where(lane1 == SM_F + h, m_state[:, h], m_prev)
    big_m = jnp.maximum(tile.scan(a, jnp.maximum, NEG_BIG), m_prev)
    ref, m_end = big_m[:, c // 2 - 1:c // 2, :], big_m[:, c - 1:c, :]
    scales = {"qm": jnp.exp(ref - big_m), "km": jnp.exp(a - ref),
              "qw": jnp.exp(m_prev - big_m), "kw": jnp.exp(a - m_end)}
    tile.store(sc["emt"], 0, SMALL_W, jnp.exp(-(bcum + big_m)))
    sc["carry"][...] = jnp.exp(m_prev - m_end)
    for h in range(HEADS):
        h0, h1 = h * ML_DK, (h + 1) * ML_DK
        qh, kh = q[:, :, h0:h1], k[:, :, h0:h1] * (ML_DK ** -0.5)
        for name, x in (("qm", qh), ("km", kh), ("qw", qh), ("kw", kh)):
            tile.store(sc[name], h0, h1, x * scales[name][:, :, SM_F + h:SM_F + h + 1])
    tile.store(sc["v"], 0, v.shape[-1], v)
    spread = jnp.where((lane >= SM_F) & (lane < SM_F + HEADS), jnp.abs(big_m - ref), 0.0)
    return _max_all(spread), bcum[:, c - 1:c, :] + m_end


def _gated_linear_units(tile, b, dk, sc, st_in, st_out, tri, emit):
    for h in range(HEADS):
        k0, k1 = h * dk, (h + 1) * dk
        att = jnp.where(tri, _dot_nt(tile.seq(sc["qs"], b, k0, k1), tile.seq(sc["ks"], b, k0, k1)), 0.0)
        vh = tile.seq(sc["v"], b, h * DV, (h + 1) * DV)
        st = st_in[b, h]
        emit(h, _dot(att.astype(BF16), vh) + _dot(tile.seq(sc["qe"], b, k0, k1), st.astype(BF16)))
        st_out[b, h] = st * sc["dec"][b, k0:k1, :] + _dot_tn(tile.seq(sc["ke"], b, k0, k1), vh)


def _mlstm_units(tile, b, sc, c_in, n_in, c_out, n_out, tri, emit):
    for h in range(HEADS):
        h0, h1 = h * ML_DK, (h + 1) * ML_DK
        att = jnp.where(tri, _dot_nt(tile.seq(sc["qm"], b, h0, h1), tile.seq(sc["km"], b, h0, h1)), 0.0)
        vh, qw, kw = (tile.seq(sc[name], b, h0, h1) for name in ("v", "qw", "kw"))
        cst, nrow = c_in[b, h], n_in[b, h]
        num = _dot(att.astype(BF16), vh) + _dot(qw, cst.astype(BF16))
        den = jnp.sum(att, axis=1, keepdims=True) + jnp.sum(qw.astype(F32) * nrow, axis=1, keepdims=True)
        emit(h, num / jnp.maximum(jnp.abs(den), tile.seq(sc["emt"], b, SM_F + h, SM_F + h + 1)))
        carry = sc["carry"][b][:, SM_F + h:SM_F + h + 1]
        c_out[b, h] = carry * cst + _dot_tn(kw, vh)
        n_out[b, h] = carry * nrow + jnp.sum(kw.astype(F32), axis=0, keepdims=True)


def _scan_body(*refs, layer, tile, zero_init, n_stacked=0):
    refs = list(refs)
    take_n = lambda n: [refs.pop(0) for _ in range(n)]
    blocks = dict(zip(BLOCK_NAMES, take_n(N_BLOCKS)))
    param_refs = take_n(N_PARAMS)
    dmat_ref, = take_n(1)
    in_states = None if zero_init else take_n(len(STATE_TAILS))
    take_n(n_stacked)
    o_ref, = take_n(1)
    out_states = take_n(len(STATE_TAILS))
    hg_sc = dict(zip(GL_SCRATCH, take_n(len(GL_SCRATCH))))
    gla_sc = dict(zip(GL_SCRATCH, take_n(len(GL_SCRATCH))))
    ml_sc = dict(zip(ML_SCRATCH, take_n(len(ML_SCRATCH))))
    raw_ref, = take_n(1)
    c, n_seq = tile.c, tile.n_seq

    if zero_init:
        in_states = out_states

        @pl.when(pl.program_id(0) == 0)
        def _():
            for r in out_states:
                r[...] = jnp.zeros_like(r)

    lb_ref, *rest = param_refs
    prm = dict(zip(PARAM_NAMES, (r[...] for r in rest)), lb=_layer_lower_bound(lb_ref[...], layer))
    lb = prm["lb"]
    col = {name: functools.partial(tile.load, ref) for name, ref in blocks.items()}
    w, wg = HEADS * HG_DK, HEADS * GLA_DK
    hg, gqk, ml = col["hg"], col["gqk"], col["ml"]

    zf = hg(w, 2 * w)
    gl_spread = jnp.maximum(
        _prepare_gated_linear(tile, _silu(hg(0, w)), (1.0 - lb) * jax.nn.sigmoid(-zf), hg(2 * w, 3 * w),
                              _hgrn_log_decay(zf, lb), hg_sc),
        _prepare_gated_linear(tile, gqk(0, wg) * (GLA_DK ** -0.5), gqk(wg, 2 * wg), col["gv"](0, HEADS * DV),
                              col["ga"](0, wg), gla_sc))
    ml_spread, m_new = _prepare_mlstm(tile, ml(0, w), ml(w, 2 * w), ml(2 * w, 3 * w),
                                      _mlstm_gates(col["sm"](0, SMALL_W), prm["ml_bias"]),
                                      in_states[4][...], ml_sc)
    worst = jnp.maximum(gl_spread * (1.0 / SAFE_LOG_RANGE), ml_spread * (1.0 / ML_SAFE_RANGE))
    tri = _pair_masks(c)[2]

    def emit_for(b):
        def emit(branch, h, val):
            c0 = branch * BRANCH_W + h * DV
            tile.seq_store(raw_ref, b, c0, c0 + DV, val)
        return emit

    def factored():
        def per_sequence(b, carry):
            emit = emit_for(b)
            _gated_linear_units(tile, b, HG_DK, hg_sc, in_states[0], out_states[0], tri, functools.partial(emit, 0))
            _gated_linear_units(tile, b, GLA_DK, gla_sc, in_states[1], out_states[1], tri, functools.partial(emit, 1))
            _mlstm_units(tile, b, ml_sc, in_states[2], in_states[3], out_states[2], out_states[3], tri,
                         functools.partial(emit, 2))
            return carry

        lax.fori_loop(0, n_seq, per_sequence, 0, unroll=tile.unroll)
        for h in range(HEADS):
            out_states[4][:, h] = jnp.broadcast_to(m_new[:, :, SM_F + h:SM_F + h + 1], (n_seq, 1, SMALL_W))

    def per_head():
        dmat = dmat_ref[...]

        def per_sequence(b, carry):
            load = {name: functools.partial(tile.seq, ref, b) for name, ref in blocks.items()}
            _sequence_chunk(load, prm, dmat, c, _state_io(b, in_states, out_states), emit_for(b))
            return carry

        lax.fori_loop(0, n_seq, per_sequence, 0)

    lax.cond(worst[0, 0, 0] < 1.0, factored, per_head)

    norms = (prm["hg_norm"], prm["gla_norm"], prm["ml_norm"])
    for branch in range(N_BRANCH):
        for h in range(HEADS):
            c0 = branch * BRANCH_W + h * DV
            tile.store(o_ref, c0, c0 + DV,
                       _finish(branch, tile.load(raw_ref, c0, c0 + DV), norms[branch], _gate_pre(col, branch, h)))


def _scan_scratch(tile):
    def gated_linear(dk):
        return ([pltpu.VMEM(tile.shape(HEADS * dk), BF16)] * 4
                + [pltpu.VMEM(tile.shape(HEADS * DV), BF16), pltpu.VMEM((tile.n_seq, HEADS * dk, 1), F32)])

    return (gated_linear(HG_DK) + gated_linear(GLA_DK) + [pltpu.VMEM(tile.shape(HEADS * ML_DK), BF16)] * 5
            + [pltpu.VMEM(tile.shape(SMALL_W), F32), pltpu.VMEM((tile.n_seq, 1, SMALL_W), F32),
               pltpu.VMEM(tile.shape(N_BRANCH * BRANCH_W), F32)])


def _param_specs(layer, depth):
    return [
        pl.BlockSpec((depth, HEADS * HG_DK), lambda *_: (0, 0)),
        pl.BlockSpec((None, 1, DV), lambda *_: (layer, 0, 0)),
        pl.BlockSpec((None, 1, DV), lambda *_: (layer, 0, 0)),
        pl.BlockSpec((None, 1, SMALL_W), lambda *_: (layer, 0, 0)),
        pl.BlockSpec((None, 1, DV), lambda *_: (layer, 0, 0)),
    ]


BLOCK_COLS = ((W_HG, COL_HG), (W_ML, COL_ML), (W_GLA_QK, COL_GLA_QK), (HEADS * DV, COL_GLA_V),
              (HEADS * DV, COL_GLA_G), (SMALL_W, COL_SMALL), (HEADS * GLA_DK, 0))


def _prompt_scan(p_all, log_a, params, dmat, layer, depth, batch, n_chunks):
    tile = _Tile(batch, CHUNK, True, batch)
    rows = batch * CHUNK
    full = lambda shape: pl.BlockSpec(shape, lambda c: (0,) * len(shape))
    state_shapes = [(batch, HEADS) + t for t in STATE_TAILS]
    return pl.pallas_call(
        functools.partial(_scan_body, layer=layer, tile=tile, zero_init=True),
        grid=(n_chunks,),
        in_specs=[pl.BlockSpec((rows, width), lambda c, blk=start // width: (c, blk)) for width, start in BLOCK_COLS]
        + _param_specs(layer, depth) + [full(dmat.shape)],
        out_specs=[pl.BlockSpec((rows, N_BRANCH * BRANCH_W), lambda c: (c, 0))] + [full(s) for s in state_shapes],
        out_shape=[jax.ShapeDtypeStruct((n_chunks * rows, N_BRANCH * BRANCH_W), BF16)]
        + [jax.ShapeDtypeStruct(s, F32) for s in state_shapes],
        scratch_shapes=_scan_scratch(tile),
        compiler_params=_cparams(("arbitrary",)),
        name="prompt_scan",
    )(*([p_all] * (N_BLOCKS - 1)), log_a, *params, dmat)


N_STACKED = 3


def _sample_scan(p_s, log_a, params, dmat, states, stacks, layer, depth):
    n_seq, seq, _ = p_s.shape
    nb = SAMPLE_NB
    tile = _Tile(nb, seq, False, 2)
    full = lambda shape: pl.BlockSpec(shape, lambda i: (0,) * len(shape))
    layer_block = lambda t: pl.BlockSpec((None, nb, HEADS) + t, lambda i: (layer, i, 0, 0, 0))
    first_stack = N_BLOCKS + N_PARAMS + 1 + len(STATE_TAILS)
    return pl.pallas_call(
        functools.partial(_scan_body, layer=layer, tile=tile, zero_init=False, n_stacked=N_STACKED),
        grid=(n_seq // nb,),
        in_specs=[pl.BlockSpec((nb, seq, width), lambda i, blk=start // width: (i, 0, blk)) for width, start in BLOCK_COLS]
        + _param_specs(layer, depth) + [full(dmat.shape)] + [layer_block(t) for t in STATE_TAILS]
        + [pl.BlockSpec(memory_space=pl.ANY)] * N_STACKED,
        out_specs=[pl.BlockSpec((nb, seq, N_BRANCH * BRANCH_W), lambda i: (i, 0, 0))]
        + [layer_block(t) for t in STATE_TAILS[:N_STACKED]]
        + [pl.BlockSpec((nb, HEADS) + t, lambda i: (i, 0, 0, 0)) for t in STATE_TAILS[N_STACKED:]],
        out_shape=[jax.ShapeDtypeStruct((n_seq, seq, N_BRANCH * BRANCH_W), BF16)]
        + [jax.ShapeDtypeStruct(s.shape, F32) for s in stacks]
        + [jax.ShapeDtypeStruct((n_seq, HEADS) + t, F32) for t in STATE_TAILS[N_STACKED:]],
        input_output_aliases={first_stack + k: 1 + k for k in range(N_STACKED)},
        scratch_shapes=_scan_scratch(tile),
        compiler_params=_cparams(("arbitrary",)),
        name="sample_scan",
    )(*([p_s] * (N_BLOCKS - 1)), log_a, *params, dmat, *states, *stacks)


SRC_GLA = W_HG
SRC_LR = SRC_GLA + W_GLA
SRC_ML = SRC_LR + GLA_RANK
SRC_IF = SRC_ML + W_ML
SRC_GATE = SRC_IF + 2 * HEADS
W_IN_MOVES = ((COL_HG, 0, W_HG), (COL_ML, SRC_ML, W_ML), (COL_GATE, SRC_GATE, W_GATE), (COL_GLA_QK, SRC_GLA, W_GLA))
W_IN_SMALL = ((SRC_LR, GLA_RANK), (SRC_IF, 2 * HEADS))
REGROUP_ROWS = 256
REGROUP_PIECE = 512


def _regroup_body(wt_ref, o_ref):
    rows = o_ref.shape[0]
    eye = _eye(rows).astype(BF16)
    transposed = lambda piece: _dot_nt(eye, piece.astype(BF16)).astype(BF16)
    for dst, src, n in W_IN_MOVES:
        for c in range(0, n, REGROUP_PIECE):
            m = min(REGROUP_PIECE, n - c)
            o_ref[:, dst + c:dst + c + m] = transposed(wt_ref[src + c:src + c + m, :])
    pieces = [wt_ref[src:src + n, :] for src, n in W_IN_SMALL]
    pad = SMALL_W - sum(n for _, n in W_IN_SMALL)
    o_ref[:, COL_SMALL:] = transposed(jnp.concatenate(pieces + [jnp.zeros((pad, rows), F32)], axis=0))


def _regroup_w_in(w_in):
    depth, d, d_in = w_in.shape
    assert d_in == sum(n for _, _, n in W_IN_MOVES) + sum(n for _, n in W_IN_SMALL) and d % REGROUP_ROWS == 0
    return pl.pallas_call(
        _regroup_body,
        grid=(depth, d // REGROUP_ROWS),
        in_specs=[pl.BlockSpec((None, d_in, REGROUP_ROWS), lambda l, i: (l, 0, i))],
        out_specs=pl.BlockSpec((None, REGROUP_ROWS, P_COLS), lambda l, i: (l, i, 0)),
        out_shape=jax.ShapeDtypeStruct((depth, d, P_COLS), BF16),
        compiler_params=_cparams(("arbitrary", "arbitrary")),
        name="regroup_w_in",
    )(jnp.swapaxes(w_in, 1, 2))


def kernel(x_prompt, x_sample, state_hgrn, state_gla, state_mlstm_C, state_mlstm_n, state_mlstm_m,
           ffn1_norm, ffn1_w_up, ffn1_w_down, mix_norm, w_in, hgrn_lb_raw, hgrn_out_norm,
           gla_w_gate_lr, gla_b_gate, gla_out_norm, mlstm_b_i, mlstm_b_f, mlstm_out_norm,
           w_branch, w_out, ffn2_norm, ffn2_w_up, ffn2_w_down, final_norm):
    depth = w_in.shape[0]
    batch, seq, _ = x_prompt.shape
    n_seq, dec_seq, _ = x_sample.shape
    assert seq % CHUNK == 0 and dec_seq % CHUNK != 0 and dec_seq & (dec_seq - 1) == 0
    assert (batch * CHUNK) % TM_TOK == 0 and n_seq * dec_seq == TM_TOK and n_seq % SAMPLE_NB == 0
    n_chunks = seq // CHUNK
    n_prompt = batch * seq

    assert TM_FFN % CHUNK == 0 and batch % (TM_FFN // CHUNK) == 0 and (n_seq * dec_seq) % TM_FFN == 0
    x = (x_prompt.reshape(batch, n_chunks, CHUNK, D_MODEL), x_sample.reshape(n_seq * dec_seq, D_MODEL))

    row3 = lambda a: a.reshape(a.shape[0], 1, a.shape[-1])
    w_all = _regroup_w_in(w_in)
    wlr_pad = jnp.pad(gla_w_gate_lr, ((0, 0), (0, SMALL_W - GLA_RANK), (0, 0))).astype(BF16)
    ml_bias = jnp.pad(jnp.concatenate([mlstm_b_i, mlstm_b_f], axis=-1),
                      ((0, 0), (SM_I, SMALL_W - SM_I - 2 * HEADS)))
    scan_params = (hgrn_lb_raw, row3(hgrn_out_norm), row3(gla_out_norm), row3(ml_bias), row3(mlstm_out_norm))
    gla_b3 = row3(gla_b_gate)
    dmat_p = jnp.asarray(_decay_matrix(CHUNK), BF16)
    dmat_s = jnp.asarray(_decay_matrix(dec_seq), BF16)
    sample_states = (state_hgrn, state_gla, state_mlstm_C,
                     state_mlstm_n.reshape(depth, n_seq, HEADS, 1, ML_DK),
                     jnp.broadcast_to(state_mlstm_m[..., None, None], (depth, n_seq, HEADS, 1, SMALL_W)))
    ffn_w = [(row3(ffn1_norm), ffn1_w_up.astype(BF16), ffn1_w_down.astype(BF16)),
             (row3(ffn2_norm), ffn2_w_up.astype(BF16), ffn2_w_down.astype(BF16))]
    w_branch_b, w_out_b, mix_norm3 = w_branch.astype(BF16), w_out.astype(BF16), row3(mix_norm)
    fin = final_norm.reshape(1, D_MODEL)

    p_states, s_small = [], []
    s_stacks = [jnp.zeros((depth, n_seq, HEADS) + t, F32) for t in STATE_TAILS[:N_STACKED]]
    for l in range(depth):
        x = _ffn(x, *ffn_w[0], fin, l, False, (batch, n_chunks), split_in=l == 0)
        p_all = _inproj(x, mix_norm3, w_all, l)
        log_a = _gla_gate(p_all, wlr_pad, gla_b3, l)
        o_p, *ps = _prompt_scan(p_all, log_a, scan_params, dmat_p, l, depth, batch, n_chunks)
        p_s = p_all[n_prompt:].reshape(n_seq, dec_seq, P_COLS)
        log_a_s = log_a[n_prompt:].reshape(n_seq, dec_seq, HEADS * GLA_DK)
        o_s, *ss = _sample_scan(p_s, log_a_s, scan_params, dmat_s, sample_states, s_stacks, l, depth)
        s_stacks = ss[:N_STACKED]
        x = _merge(x, o_p, o_s.reshape(n_seq * dec_seq, N_BRANCH * BRANCH_W), p_all, w_branch_b, w_out_b, l)
        last = l == depth - 1
        x = _ffn(x, *ffn_w[1], fin, l, last, (batch, n_chunks), split_out=last)
        p_states.append(ps)
        s_small.append(ss[N_STACKED:])

    y_prompt = x[0].reshape(batch, seq, D_MODEL)
    y_sample = x[1].reshape(n_seq, dec_seq, D_MODEL)

    stack = lambda states, i: jnp.stack([st[i] for st in states])
    vectors = lambda mn, mm: (mn[..., 0, :], mm[..., 0, 0])
    prompt_out = tuple(stack(p_states, i) for i in range(N_STACKED)) + vectors(stack(p_states, 3), stack(p_states, 4))
    sample_out = tuple(s_stacks) + vectors(stack(s_small, 0), stack(s_small, 1))
    return (y_prompt, y_sample) + prompt_out + sample_out
```
